```python
import math
import jax
import jax.numpy as jnp
from jax import lax
import numpy as np

D_MODEL = 1024
BATCH = 8
SEQ = 4096
DEPTH = 2

N_MEM = 256
EPS = 1e-6
ROPE_THETA = 10000.0
ATT_HEADS = 8
ATT_HEAD_DIM = 64
ATT_WIDTH = ATT_HEADS * ATT_HEAD_DIM
POOL_WIDTH = D_MODEL - ATT_WIDTH
POOL_WINDOWS = (2, 4, 8, 16)
POOL_GROUPS = len(POOL_WINDOWS)
POOL_GROUP_DIM = POOL_WIDTH // POOL_GROUPS
HY_IN_DIM = 3 * ATT_WIDTH + POOL_WIDTH
MOBA_BLOCK = 256
MOBA_TOPK = 3
Q_CHUNK = 128
SSD_EXPAND = 2
SSD_D_INNER = SSD_EXPAND * D_MODEL
SSD_HEAD_DIM = 64
SSD_HEADS = SSD_D_INNER // SSD_HEAD_DIM
SSD_GROUPS = 4
SSD_STATE = 128
SSD_CONV = 4
SSD_CHUNK = 128
SSD_CONV_DIM = SSD_D_INNER + 2 * SSD_GROUPS * SSD_STATE
SSD_IN_DIM = SSD_D_INNER + SSD_CONV_DIM + SSD_HEADS
DT_MIN = 0.001
DT_MAX = 0.1
XA_HEADS = 4
XA_HEAD_DIM = D_MODEL // XA_HEADS
D_FF = 2816
N_EXPERTS = 8
TOP_K = 2
D_FF_EXPERT = 3584
MOE_BLOCK = 256

kernel_name = 'hybrid_moba_pool_ssd_moe'


def _rmsnorm(x, gain):
    xf = x.astype(jnp.float32)
    y = xf * lax.rsqrt(jnp.mean(xf * xf, axis=-1, keepdims=True) + EPS)
    return (y * gain.astype(jnp.float32)).astype(x.dtype)


def _rope(x, positions):
    half = x.shape[-1] // 2
    inv_freq = ROPE_THETA ** (-jnp.arange(half, dtype=jnp.float32) / half)
    ang = positions.astype(jnp.float32)[:, None] * inv_freq[None, :]
    cos = jnp.cos(ang).astype(x.dtype)
    sin = jnp.sin(ang).astype(x.dtype)
    x1, x2 = x[..., :half], x[..., half:]
    return jnp.concatenate([x1 * cos - x2 * sin, x2 * cos + x1 * sin], axis=-1)


def _moba_attention(q, k, v):
    b, h, s, dh = q.shape
    n_blk = -(-s // MOBA_BLOCK)
    pad = n_blk * MOBA_BLOCK - s
    kp = jnp.pad(k, ((0, 0), (0, 0), (0, pad), (0, 0)))
    vp = jnp.pad(v, ((0, 0), (0, 0), (0, pad), (0, 0)))
    k_blocks = kp.reshape(b, h, n_blk, MOBA_BLOCK, dh)
    v_blocks = vp.reshape(b, h, n_blk, MOBA_BLOCK, dh)
    k_mean = jnp.mean(k_blocks.astype(jnp.float32), axis=3)
    scale = dh ** -0.5
    n_gate = max(n_blk, MOBA_TOPK)
    bi = jnp.arange(b)[:, None, None, None]
    hi = jnp.arange(h)[None, :, None, None]
    blk_ids = jnp.arange(n_blk)
    slot_ids = jnp.arange(MOBA_TOPK)
    local = jnp.arange(MOBA_BLOCK)

    def chunk(c):
        q0 = c * Q_CHUNK
        own = q0 // MOBA_BLOCK
        qc = lax.dynamic_slice_in_dim(q, q0, Q_CHUNK, axis=2)
        qpos = q0 + jnp.arange(Q_CHUNK)
        gate = jnp.einsum('bhqd,bhnd->bhqn', qc.astype(jnp.float32), k_mean)
        gate = jnp.where(blk_ids < own, gate, -jnp.inf)
        if n_gate > n_blk:
            gate = jnp.pad(gate, ((0, 0), (0, 0), (0, 0), (0, n_gate - n_blk)),
                           constant_values=-jnp.inf)
        _, idx = lax.top_k(gate, MOBA_TOPK)
        idx = jnp.minimum(idx, n_blk - 1)
        slot_ok = slot_ids < own
        kg = k_blocks[bi, hi, idx]
        vg = v_blocks[bi, hi, idx]
        s_sel = jnp.einsum('bhqd,bhqjkd->bhqjk', qc, kg).astype(jnp.float32) * scale
        s_sel = jnp.where(slot_ok[:, None], s_sel, -jnp.inf)
        s_sel = s_sel.reshape(b, h, Q_CHUNK, MOBA_TOPK * MOBA_BLOCK)
        k_own = lax.dynamic_slice_in_dim(kp, own * MOBA_BLOCK, MOBA_BLOCK, axis=2)
        v_own = lax.dynamic_slice_in_dim(vp, own * MOBA_BLOCK, MOBA_BLOCK, axis=2)
        kpos = own * MOBA_BLOCK + local
        s_own = jnp.einsum('bhqd,bhkd->bhqk', qc, k_own).astype(jnp.float32) * scale
        s_own = jnp.where(kpos[None, :] <= qpos[:, None], s_own, -jnp.inf)
        p = jax.nn.softmax(jnp.concatenate([s_sel, s_own], axis=-1), axis=-1).astype(v.dtype)
        p_sel = p[..., :MOBA_TOPK * MOBA_BLOCK].reshape(b, h, Q_CHUNK, MOBA_TOPK, MOBA_BLOCK)
        p_own = p[..., MOBA_TOPK * MOBA_BLOCK:]
        return (jnp.einsum('bhqjk,bhqjkd->bhqd', p_sel, vg)
                + jnp.einsum('bhqk,bhkd->bhqd', p_own, v_own))

    outs = lax.map(chunk, jnp.arange(s // Q_CHUNK))
    return outs.transpose(1, 2, 0, 3, 4).reshape(b, h, s, dh)


def _multiscale_pool(u, w_pool, pool_scale):
    b, s, _ = u.shape
    uf = u.astype(jnp.float32)
    cs = jnp.concatenate([jnp.zeros((b, 1, POOL_WIDTH), jnp.float32),
                          jnp.cumsum(uf, axis=1)], axis=1)
    t = jnp.arange(s)
    groups = []
    for g, win in enumerate(POOL_WINDOWS):
        sl = slice(g * POOL_GROUP_DIM, (g + 1) * POOL_GROUP_DIM)
        cs_g = cs[..., sl]
        lo = jnp.maximum(t + 1 - win, 0)
        cnt = jnp.minimum(t + 1, win).astype(jnp.float32)
        groups.append((cs_g[:, 1:] - cs_g[:, lo]) / cnt[None, :, None] - uf[..., sl])
    pooled = jnp.stack(groups, axis=2).astype(u.dtype)
    mixed = jnp.einsum('bsgc,gcd->bsgd', pooled, w_pool).reshape(b, s, POOL_WIDTH)
    return mixed * pool_scale


def _attn_pool_mixer(hn, positions, w_in, q_norm, k_norm, w_pool, pool_scale, w_out):
    b, s, _ = hn.shape
    proj = hn @ w_in
    q, k, v, u = jnp.split(proj, [ATT_WIDTH, 2 * ATT_WIDTH, 3 * ATT_WIDTH], axis=-1)

    def heads(t):
        return t.reshape(b, s, ATT_HEADS, ATT_HEAD_DIM).transpose(0, 2, 1, 3)

    q = _rope(_rmsnorm(heads(q), q_norm), positions)
    k = _rope(_rmsnorm(heads(k), k_norm), positions)
    a = _moba_attention(q, k, heads(v)).transpose(0, 2, 1, 3).reshape(b, s, ATT_WIDTH)
    p = _multiscale_pool(u, w_pool, pool_scale)
    return jnp.concatenate([a, p.astype(a.dtype)], axis=-1) @ w_out


def _ssd_chunked(x, dt, a, bmat, cmat):
    b, s, h, p = x.shape
    l = SSD_CHUNK
    c = s // l
    g = SSD_GROUPS
    r = h // g
    n = bmat.shape[-1]
    xdt = (x.astype(jnp.float32) * dt[..., None]).reshape(b, c, l, g, r, p)
    da = (dt * a).reshape(b, c, l, g, r)
    bm = bmat.astype(jnp.float32).reshape(b, c, l, g, n)
    cm = cmat.astype(jnp.float32).reshape(b, c, l, g, n)
    a_cum = jnp.cumsum(da, axis=2)
    causal = jnp.tril(jnp.ones((l, l), bool))
    seg = a_cum[:, :, :, None] - a_cum[:, :, None, :]
    decay = jnp.exp(jnp.where(causal[None, None, :, :, None, None], seg, -jnp.inf))
    cb = jnp.einsum('bctgn,bcsgn->bctsg', cm, bm)
    y_diag = jnp.einsum('bctsg,bctsgr,bcsgrp->bctgrp', cb, decay, xdt)
    to_end = jnp.exp(a_cum[:, :, -1:] - a_cum)
    states = jnp.einsum('bcsgn,bcsgr,bcsgrp->bcgrpn', bm, to_end, xdt)
    chunk_decay = jnp.exp(a_cum[:, :, -1])

    def carry_state(state, inp):
        st, dec = inp
        return state * dec[..., None, None] + st, state

    init = jnp.zeros((b, g, r, p, n), jnp.float32)
    _, s_in = lax.scan(carry_state, init,
                       (jnp.moveaxis(states, 1, 0), jnp.moveaxis(chunk_decay, 1, 0)))
    s_in = jnp.moveaxis(s_in, 0, 1)
    y_off = jnp.einsum('bctgn,bcgrpn,bctgr->bctgrp', cm, s_in, jnp.exp(a_cum))
    return (y_diag + y_off).reshape(b, s, h, p)


def _ssd_mixer(hn, w_in, conv_w, conv_b, dt_bias, a_log, d_skip, norm_g, w_out):
    b, s, _ = hn.shape
    proj = hn @ w_in
    z, xbc, dt = jnp.split(proj, [SSD_D_INNER, SSD_D_INNER + SSD_CONV_DIM], axis=-1)
    xpad = jnp.pad(xbc, ((0, 0), (SSD_CONV - 1, 0), (0, 0)))
    conv = xpad[:, 0:s] * conv_w[0]
    for i in range(1, SSD_CONV):
        conv = conv + xpad[:, i:i + s] * conv_w[i]
    xbc = jax.nn.silu(conv + conv_b)
    xs, bmat, cmat = jnp.split(xbc, [SSD_D_INNER, SSD_D_INNER + SSD_GROUPS * SSD_STATE], axis=-1)
    xs = xs.reshape(b, s, SSD_HEADS, SSD_HEAD_DIM)
    bmat = bmat.reshape(b, s, SSD_GROUPS, SSD_STATE)
    cmat = cmat.reshape(b, s, SSD_GROUPS, SSD_STATE)
    dt = jax.nn.softplus(dt.astype(jnp.float32) + dt_bias.astype(jnp.float32))
    a = -jnp.exp(a_log.astype(jnp.float32))
    y = _ssd_chunked(xs, dt, a, bmat, cmat)
    y = (y + d_skip.astype(jnp.float32)[:, None] * xs.astype(jnp.float32)).astype(hn.dtype)
    y = (y.reshape(b, s, SSD_D_INNER) * jax.nn.silu(z)).reshape(b, s, SSD_GROUPS, -1)
    y = _rmsnorm(y, norm_g.reshape(SSD_GROUPS, -1)).reshape(b, s, SSD_D_INNER)
    return y @ w_out


def _memory_cross_attention(hn, mem_n, w_q, w_kv, q_norm, k_norm, w_o):
    b, s, _ = hn.shape
    m = mem_n.shape[1]
    q = _rmsnorm((hn @ w_q).reshape(b, s, XA_HEADS, XA_HEAD_DIM), q_norm)
    k, v = jnp.split(mem_n @ w_kv, 2, axis=-1)
    k = _rmsnorm(k.reshape(b, m, XA_HEADS, XA_HEAD_DIM), k_norm)
    v = v.reshape(b, m, XA_HEADS, XA_HEAD_DIM)
    sc = jnp.einsum('bshd,bmhd->bhsm', q, k).astype(jnp.float32) * (XA_HEAD_DIM ** -0.5)
    p = jax.nn.softmax(sc, axis=-1).astype(v.dtype)
    o = jnp.einsum('bhsm,bmhd->bshd', p, v).reshape(b, s, D_MODEL)
    return o @ w_o


def _swiglu(hn, w_gate, w_up, w_down):
    return (jax.nn.silu(hn @ w_gate) * (hn @ w_up)) @ w_down


def _moe_swiglu(hn, w_router, w_gate, w_up, w_down):
    b, s, d = hn.shape
    n = b * s
    xt = hn.reshape(n, d)
    logits = (xt @ w_router).astype(jnp.float32)
    top_logit, top_e = lax.top_k(logits, TOP_K)
    gates = jax.nn.softmax(top_logit, axis=-1)
    e_flat = top_e.reshape(-1)
    tok_flat = jnp.arange(n * TOP_K, dtype=jnp.int32) // TOP_K
    g_flat = gates.reshape(-1)
    order = jnp.argsort(e_flat)
    e_sorted = e_flat[order]
    tok_sorted = tok_flat[order]
    g_sorted = g_flat[order]
    counts = jnp.zeros((N_EXPERTS,), jnp.int32).at[e_flat].add(1)
    start = jnp.cumsum(counts) - counts
    padded = (counts + MOE_BLOCK - 1) // MOE_BLOCK * MOE_BLOCK
    pad_end = jnp.cumsum(padded)
    pad_start = pad_end - padded
    rank = jnp.arange(n * TOP_K, dtype=jnp.int32) - start[e_sorted]
    dest = pad_start[e_sorted] + rank
    n_rows = (-(-(n * TOP_K) // MOE_BLOCK) + N_EXPERTS) * MOE_BLOCK
    n_blocks = n_rows // MOE_BLOCK
    xbuf = jnp.zeros((n_rows, d), hn.dtype).at[dest].set(xt[tok_sorted])
    blk_start = jnp.arange(n_blocks, dtype=jnp.int32) * MOE_BLOCK
    blk_e = jnp.minimum(jnp.sum(blk_start[:, None] >= pad_end[None, :], axis=1), N_EXPERTS - 1)

    def expert_block(args):
        xb, e = args
        return (jax.nn.silu(xb @ w_gate[e]) * (xb @ w_up[e])) @ w_down[e]

    ybuf = lax.map(expert_block, (xbuf.reshape(n_blocks, MOE_BLOCK, d), blk_e)).reshape(n_rows, d)
    y = jax.ops.segment_sum(ybuf[dest] * g_sorted[:, None].astype(ybuf.dtype), tok_sorted,
                            num_segments=n)
    return y.reshape(b, s, d)


def setup_inputs(seed: int = 0) -> dict:
    key = jax.random.key(seed)
    ks = iter(jax.random.split(key, 64))
    f32 = jnp.float32
    ne = (DEPTH + 1) // 2
    no = DEPTH // 2

    def w(shape, fan_in):
        return jax.random.normal(next(ks), shape, f32) * (fan_in ** -0.5)

    def gain(shape, s=0.02):
        return 1.0 + s * jax.random.normal(next(ks), shape, f32)

    x = jax.random.normal(next(ks), (BATCH, SEQ, D_MODEL), f32)
    mem = jax.random.normal(next(ks), (BATCH, N_MEM, D_MODEL), f32)
    positions = jnp.arange(SEQ, dtype=jnp.int32)
    dt0 = jnp.exp(jax.random.uniform(next(ks), (no, SSD_HEADS), f32,
                                     math.log(DT_MIN), math.log(DT_MAX)))
    ssd_dt_bias = dt0 + jnp.log(-jnp.expm1(-dt0))
    ssd_a_log = jnp.log(jax.random.uniform(next(ks), (no, SSD_HEADS), f32, 1.0, 16.0))
    return {
        'x': x,
        'mem': mem,
        'positions': positions,
        'mix_norm': gain((DEPTH, D_MODEL)),
        'xa_norm': gain((DEPTH, D_MODEL)),
        'mem_norm': gain((DEPTH, D_MODEL)),
        'ffn_norm': gain((DEPTH, D_MODEL)),
        'xa_wq': w((DEPTH, D_MODEL, D_MODEL), D_MODEL),
        'xa_wkv': w((DEPTH, D_MODEL, 2 * D_MODEL), D_MODEL),
        'xa_q_norm': gain((DEPTH, XA_HEAD_DIM)),
        'xa_k_norm': gain((DEPTH, XA_HEAD_DIM)),
        'xa_wo': w((DEPTH, D_MODEL, D_MODEL), D_MODEL),
        'hy_w_in': w((ne, D_MODEL, HY_IN_DIM), D_MODEL),
        'hy_q_norm': gain((ne, ATT_HEAD_DIM)),
        'hy_k_norm': gain((ne, ATT_HEAD_DIM)),
        'pool_w': w((ne, POOL_GROUPS, POOL_GROUP_DIM, POOL_GROUP_DIM), POOL_GROUP_DIM),
        'pool_scale': gain((ne, POOL_WIDTH), 0.1),
        'hy_w_out': w((ne, D_MODEL, D_MODEL), D_MODEL),
        'ffn_w_gate': w((ne, D_MODEL, D_FF), D_MODEL),
        'ffn_w_up': w((ne, D_MODEL, D_FF), D_MODEL),
        'ffn_w_down': w((ne, D_FF, D_MODEL), D_FF),
        'ssd_w_in': w((no, D_MODEL, SSD_IN_DIM), D_MODEL),
        'ssd_conv_w': jax.random.uniform(next(ks), (no, SSD_CONV, SSD_CONV_DIM), f32, -0.5, 0.5),
        'ssd_conv_b': 0.01 * jax.random.normal(next(ks), (no, SSD_CONV_DIM), f32),
        'ssd_dt_bias': ssd_dt_bias,
        'ssd_a_log': ssd_a_log,
        'ssd_d': gain((no, SSD_HEADS), 0.1),
        'ssd_norm': gain((no, SSD_D_INNER)),
        'ssd_w_out': w((no, SSD_D_INNER, D_MODEL), SSD_D_INNER),
        'moe_router': w((no, D_MODEL, N_EXPERTS), D_MODEL),
        'moe_w_gate': w((no, N_EXPERTS, D_MODEL, D_FF_EXPERT), D_MODEL),
        'moe_w_up': w((no, N_EXPERTS, D_MODEL, D_FF_EXPERT), D_MODEL),
        'moe_w_down': w((no, N_EXPERTS, D_FF_EXPERT, D_MODEL), D_FF_EXPERT),
    }


def reference(x, mem, positions, mix_norm, xa_norm, mem_norm, ffn_norm, xa_wq, xa_wkv,
              xa_q_norm, xa_k_norm, xa_wo, hy_w_in, hy_q_norm, hy_k_norm, pool_w, pool_scale,
              hy_w_out, ffn_w_gate, ffn_w_up, ffn_w_down, ssd_w_in, ssd_conv_w, ssd_conv_b,
              ssd_dt_bias, ssd_a_log, ssd_d, ssd_norm, ssd_w_out, moe_router, moe_w_gate,
              moe_w_up, moe_w_down):
    h = x
    for layer in range(DEPTH):
        j = layer // 2
        hn = _rmsnorm(h, mix_norm[layer])
        if layer % 2 == 0:
            h = h + _attn_pool_mixer(hn, positions, hy_w_in[j], hy_q_norm[j], hy_k_norm[j],
                                     pool_w[j], pool_scale[j], hy_w_out[j])
        else:
            h = h + _ssd_mixer(hn, ssd_w_in[j], ssd_conv_w[j], ssd_conv_b[j], ssd_dt_bias[j],
                               ssd_a_log[j], ssd_d[j], ssd_norm[j], ssd_w_out[j])
        h = h + _memory_cross_attention(_rmsnorm(h, xa_norm[layer]),
                                        _rmsnorm(mem, mem_norm[layer]), xa_wq[layer],
                                        xa_wkv[layer], xa_q_norm[layer], xa_k_norm[layer],
                                        xa_wo[layer])
        hn = _rmsnorm(h, ffn_norm[layer])
        if layer % 2 == 0:
            h = h + _swiglu(hn, ffn_w_gate[j], ffn_w_up[j], ffn_w_down[j])
        else:
            h = h + _moe_swiglu(hn, moe_router[j], moe_w_gate[j], moe_w_up[j], moe_w_down[j])
    return h
```

```python
import functools
import math

import jax
import jax.numpy as jnp
from jax import lax
from jax.experimental import pallas as pl
from jax.experimental.pallas import tpu as pltpu

F32 = jnp.float32
BF16 = jnp.bfloat16
HIGHEST = lax.Precision.HIGHEST

EPS = 1e-6
ROPE_THETA = 10000.0
LANES = 128
SUBLANES = 8
VMEM_LIMIT_BYTES = 56 * 1024 * 1024

ATT_HEADS = 8
ATT_HEAD_DIM = 64
ATT_WIDTH = ATT_HEADS * ATT_HEAD_DIM
POOL_WINDOWS = (2, 4, 8, 16)
POOL_GROUP_DIM = 128
POOL_HALO = 16
MOBA_BLOCK = 256
MOBA_TOPK = 3
NEG_BIG = -1e30

SSD_D_INNER = 2048
SSD_HEAD_DIM = 64
SSD_HEADS = 32
SSD_GROUPS = 4
SSD_STATE = 128
SSD_CONV = 4
SSD_CHUNK = 128
SSD_CONV_DIM = SSD_D_INNER + 2 * SSD_GROUPS * SSD_STATE
SSD_DT_PAD = 256
SSD_PROJ_TILE = 896

XA_HEADS = 4
XA_HEAD_DIM = 256

N_EXPERTS = 8
TOP_K = 2
MOE_ROWS = 512
MOE_FF_CHUNK = 896
COMBINE_TOKENS = 256


def _cparams(*sem):
    return pltpu.CompilerParams(dimension_semantics=sem, vmem_limit_bytes=VMEM_LIMIT_BYTES)


def _rms(x, g):
    return x * lax.rsqrt(jnp.mean(x * x, axis=-1, keepdims=True) + EPS) * g


def _silu(x):
    return x / (1.0 + jnp.exp(-x))


def _iota(shape, dim):
    return lax.broadcasted_iota(jnp.int32, shape, dim)


def _norm_matmul_body(x_ref, g_ref, w_ref, o_ref, xn_ref):
    @pl.when(pl.program_id(1) == 0)
    def _():
        xn_ref[...] = _rms(x_ref[...], g_ref[...]).astype(BF16)

    o_ref[...] = jnp.dot(xn_ref[...], w_ref[...], preferred_element_type=F32)


def _norm_matmul(x, gain, w, tm, tn):
    n, d = x.shape
    f = w.shape[1]
    return pl.pallas_call(
        _norm_matmul_body,
        out_shape=jax.ShapeDtypeStruct((n, f), F32),
        grid=(n // tm, f // tn),
        in_specs=[pl.BlockSpec((tm, d), lambda i, j: (i, 0)),
                  pl.BlockSpec((1, d), lambda i, j: (0, 0)),
                  pl.BlockSpec((d, tn), lambda i, j: (0, j))],
        out_specs=pl.BlockSpec((tm, tn), lambda i, j: (i, j)),
        scratch_shapes=[pltpu.VMEM((tm, d), BF16)],
        compiler_params=_cparams("parallel", "arbitrary"),
        name="norm_matmul",
    )(x, gain.reshape(1, d), w)


def _matmul_residual_body(n_pairs, res_ref, *refs):
    o_ref = refs[-1]
    acc = res_ref[...]
    for p in range(n_pairs):
        acc = acc + jnp.dot(refs[2 * p][...].astype(BF16), refs[2 * p + 1][...],
                            preferred_element_type=F32)
    o_ref[...] = acc


def _matmul_residual(res, pairs, tm):
    n, d = res.shape
    in_specs = [pl.BlockSpec((tm, d), lambda i: (i, 0))]
    args = [res]
    for a, w in pairs:
        in_specs.append(pl.BlockSpec((tm, a.shape[1]), lambda i: (i, 0)))
        in_specs.append(pl.BlockSpec(w.shape, lambda i: (0, 0)))
        args += [a, w]
    return pl.pallas_call(
        functools.partial(_matmul_residual_body, len(pairs)),
        out_shape=jax.ShapeDtypeStruct((n, d), F32),
        grid=(n // tm,),
        in_specs=in_specs,
        out_specs=pl.BlockSpec((tm, d), lambda i: (i, 0)),
        compiler_params=_cparams("parallel"),
        name="matmul_residual",
    )(*args)


def _qk_prep_body(q_ref, k_ref, cos_ref, sin_ref, qg_ref, kg_ref, qo_ref, ko_ref, km_ref):
    lane = _iota((1, LANES), 1)
    first_half = (lane % ATT_HEAD_DIM) < (ATT_HEAD_DIM // 2)
    r = _iota((LANES, LANES), 0) // ATT_HEAD_DIM
    c = _iota((LANES, LANES), 1) // ATT_HEAD_DIM
    head_mean = jnp.where(r == c, 1.0 / ATT_HEAD_DIM, 0.0).astype(F32)
    cos = cos_ref[...]
    sin = sin_ref[...]

    def prep(x, g):
        ms = jnp.dot(x * x, head_mean, precision=HIGHEST, preferred_element_type=F32)
        xn = x * lax.rsqrt(ms + EPS) * g
        half = ATT_HEAD_DIM // 2
        partner = jnp.where(first_half, pltpu.roll(xn, LANES - half, 1), pltpu.roll(xn, half, 1))
        return xn * cos + partner * sin

    qo_ref[0] = prep(q_ref[0], qg_ref[...])
    kk = prep(k_ref[0], kg_ref[...])
    ko_ref[0] = kk
    km_ref[0, 0] = jnp.mean(kk, axis=0, keepdims=True)


def _qk_prep(proj, cos, sin, qg, kg):
    b, s, _ = proj.shape
    nb = s // MOBA_BLOCK
    n_pairs = ATT_WIDTH // LANES
    blk = (1, MOBA_BLOCK, LANES)
    qn, kn, km = pl.pallas_call(
        _qk_prep_body,
        out_shape=[jax.ShapeDtypeStruct((b, s, ATT_WIDTH), F32),
                   jax.ShapeDtypeStruct((b, s, ATT_WIDTH), F32),
                   jax.ShapeDtypeStruct((b, nb, 1, ATT_WIDTH), F32)],
        grid=(b, nb, n_pairs),
        in_specs=[pl.BlockSpec(blk, lambda bi, i, p: (bi, i, p)),
                  pl.BlockSpec(blk, lambda bi, i, p: (bi, i, n_pairs + p)),
                  pl.BlockSpec((MOBA_BLOCK, LANES), lambda bi, i, p: (i, 0)),
                  pl.BlockSpec((MOBA_BLOCK, LANES), lambda bi, i, p: (i, 0)),
                  pl.BlockSpec((1, LANES), lambda bi, i, p: (0, 0)),
                  pl.BlockSpec((1, LANES), lambda bi, i, p: (0, 0))],
        out_specs=[pl.BlockSpec(blk, lambda bi, i, p: (bi, i, p)),
                   pl.BlockSpec(blk, lambda bi, i, p: (bi, i, p)),
                   pl.BlockSpec((1, 1, 1, LANES), lambda bi, i, p: (bi, i, 0, p))],
        compiler_params=_cparams("parallel", "parallel", "parallel"),
        name="moba_qk_prep",
    )(proj, proj, cos, sin, qg, kg)
    return qn, kn, km.reshape(b, nb, ATT_WIDTH)


def _moba_body(q_ref, k_ref, v_ref, km_ref, o_ref):
    i = pl.program_id(2)
    bs = MOBA_BLOCK
    q = q_ref[0]
    km = km_ref[0]
    nb = km.shape[0]
    scale = ATT_HEAD_DIM ** -0.5
    lane = _iota((1, LANES), 1)
    blk = _iota((1, nb), 1)
    causal = _iota((bs, bs), 1) <= _iota((bs, bs), 0)
    nt = (((1,), (1,)), ((), ()))

    def k_block(j):
        start = pl.multiple_of(j * bs, bs)
        return (k_ref[0, pl.ds(start, bs), :].astype(BF16),
                v_ref[0, pl.ds(start, bs), :].astype(BF16))

    k_own, v_own = k_block(i)
    q_heads, sel_bias, carry = [], [], []
    for hh in range(2):
        qh = jnp.where((lane // ATT_HEAD_DIM) == hh, q, 0.0)
        gate = lax.dot_general(qh, km, nt, precision=HIGHEST, preferred_element_type=F32)
        gate = jnp.where(blk < i, gate, -jnp.inf)
        cnt = jnp.zeros((bs, nb), F32)
        for jp in range(nb):
            gj = gate[:, jp:jp + 1]
            beats = jnp.where(jp < blk, jnp.where(gj >= gate, 1.0, 0.0),
                              jnp.where(gj > gate, 1.0, 0.0))
            cnt = cnt + beats
        keep = jnp.logical_and(blk < i, cnt < MOBA_TOPK)
        sel_bias.append(jnp.where(keep, 0.0, NEG_BIG))
        qb = qh.astype(BF16)
        q_heads.append(qb)
        s = lax.dot_general(qb, k_own, nt, preferred_element_type=F32) * scale
        s = jnp.where(causal, s, NEG_BIG)
        m = jnp.max(s, axis=-1, keepdims=True)
        p = jnp.exp(s - m)
        l = jnp.sum(p, axis=-1, keepdims=True)
        acc = jnp.dot(p.astype(BF16), v_own, preferred_element_type=F32)
        carry += [m, l, acc]

    def past_block(j, carry):
        k_j, v_j = k_block(j)
        out = []
        for hh in range(2):
            m, l, acc = carry[3 * hh:3 * hh + 3]
            bias = jnp.sum(jnp.where(blk == j, sel_bias[hh], 0.0), axis=-1, keepdims=True)
            s = lax.dot_general(q_heads[hh], k_j, nt, preferred_element_type=F32) * scale + bias
            m_new = jnp.maximum(m, jnp.max(s, axis=-1, keepdims=True))
            alpha = jnp.exp(m - m_new)
            p = jnp.exp(s - m_new)
            l = alpha * l + jnp.sum(p, axis=-1, keepdims=True)
            acc = alpha * acc + jnp.dot(p.astype(BF16), v_j, preferred_element_type=F32)
            out += [m_new, l, acc]
        return tuple(out)

    carry = lax.fori_loop(0, i, past_block, tuple(carry))
    o0 = carry[2] / carry[1]
    o1 = carry[5] / carry[4]
    o_ref[0] = jnp.where(lane < ATT_HEAD_DIM, o0, o1)


def _moba_attention(qn, kn, proj, kmean):
    b, s, _ = qn.shape
    nb = s // MOBA_BLOCK
    n_pairs = ATT_WIDTH // LANES
    v_col0 = 2 * n_pairs
    return pl.pallas_call(
        _moba_body,
        out_shape=jax.ShapeDtypeStruct((b, s, ATT_WIDTH), F32),
        grid=(b, n_pairs, nb),
        in_specs=[pl.BlockSpec((1, MOBA_BLOCK, LANES), lambda bi, p, i: (bi, i, p)),
                  pl.BlockSpec((1, s, LANES), lambda bi, p, i: (bi, 0, p)),
                  pl.BlockSpec((1, s, LANES), lambda bi, p, i: (bi, 0, v_col0 + p)),
                  pl.BlockSpec((1, nb, LANES), lambda bi, p, i: (bi, 0, p))],
        out_specs=pl.BlockSpec((1, MOBA_BLOCK, LANES), lambda bi, p, i: (bi, i, p)),
        compiler_params=_cparams("parallel", "parallel", "arbitrary"),
        name="moba_attention",
    )(qn, kn, proj, kmean)


def _pool_body(ts, u_ref, halo_ref, w_ref, sc_ref, o_ref, ext_ref):
    i = pl.program_id(1)
    u = u_ref[0]
    ext_ref[0:POOL_HALO, :] = jnp.where(i > 0, halo_ref[0], 0.0)
    ext_ref[POOL_HALO:POOL_HALO + ts, :] = u
    t = i * ts + _iota((ts, 1), 0)
    outs = []
    for g, win in enumerate(POOL_WINDOWS):
        cols = slice(g * POOL_GROUP_DIM, (g + 1) * POOL_GROUP_DIM)
        ug = u[:, cols]
        acc = ug
        for k in range(1, win):
            acc = acc + ext_ref[POOL_HALO - k:POOL_HALO - k + ts, cols]
        cnt = jnp.minimum(t + 1, win).astype(F32)
        pooled = acc / cnt - ug
        outs.append(jnp.dot(pooled.astype(BF16), w_ref[g], preferred_element_type=F32))
    o_ref[0] = jnp.concatenate(outs, axis=1) * sc_ref[...]


def _multiscale_pool(proj, w_pool, pool_scale, ts):
    b, s, f = proj.shape
    width = len(POOL_WINDOWS) * POOL_GROUP_DIM
    assert max(POOL_WINDOWS) <= POOL_HALO and f % width == 0
    col = f // width - 1
    halo_per_tile = ts // POOL_HALO
    return pl.pallas_call(
        functools.partial(_pool_body, ts),
        out_shape=jax.ShapeDtypeStruct((b, s, width), F32),
        grid=(b, s // ts),
        in_specs=[pl.BlockSpec((1, ts, width), lambda bi, i: (bi, i, col)),
                  pl.BlockSpec((1, POOL_HALO, width),
                               lambda bi, i: (bi, jnp.maximum(i * halo_per_tile - 1, 0), col)),
                  pl.BlockSpec(w_pool.shape, lambda bi, i: (0, 0, 0)),
                  pl.BlockSpec((1, width), lambda bi, i: (0, 0))],
        out_specs=pl.BlockSpec((1, ts, width), lambda bi, i: (bi, i, 0)),
        scratch_shapes=[pltpu.VMEM((POOL_HALO + ts, width), F32)],
        compiler_params=_cparams("parallel", "parallel"),
        name="multiscale_pool",
    )(proj, proj, w_pool, pool_scale.reshape(1, width))


def _xattn_body(h_ref, kv_ref, g_ref, wq_ref, qg_ref, kg_ref, wo_ref, o_ref):
    h = h_ref[0]
    d = h.shape[-1]
    kv = kv_ref[0]
    hn = _rms(h, g_ref[...]).astype(BF16)
    q = jnp.dot(hn, wq_ref[...], preferred_element_type=F32)
    scale = XA_HEAD_DIM ** -0.5
    nt = (((1,), (1,)), ((), ()))
    outs = []
    for hh in range(XA_HEADS):
        cols = slice(hh * XA_HEAD_DIM, (hh + 1) * XA_HEAD_DIM)
        qh = _rms(q[:, cols], qg_ref[...]).astype(BF16)
        kh = _rms(kv[:, cols], kg_ref[...]).astype(BF16)
        vh = kv[:, d + hh * XA_HEAD_DIM:d + (hh + 1) * XA_HEAD_DIM].astype(BF16)
        s = lax.dot_general(qh, kh, nt, preferred_element_type=F32) * scale
        m = jnp.max(s, axis=-1, keepdims=True)
        p = jnp.exp(s - m)
        p = p / jnp.sum(p, axis=-1, keepdims=True)
        outs.append(jnp.dot(p.astype(BF16), vh, preferred_element_type=F32))
    o = jnp.concatenate(outs, axis=1).astype(BF16)
    o_ref[0] = h + jnp.dot(o, wo_ref[...], preferred_element_type=F32)


def _cross_attention(h, kv, gain, wq, q_gain, k_gain, wo, ts):
    b, s, d = h.shape
    m = kv.shape[1]
    return pl.pallas_call(
        _xattn_body,
        out_shape=jax.ShapeDtypeStruct((b, s, d), F32),
        grid=(b, s // ts),
        in_specs=[pl.BlockSpec((1, ts, d), lambda bi, i: (bi, i, 0)),
                  pl.BlockSpec((1, m, 2 * d), lambda bi, i: (bi, 0, 0)),
                  pl.BlockSpec((1, d), lambda bi, i: (0, 0)),
                  pl.BlockSpec((d, d), lambda bi, i: (0, 0)),
                  pl.BlockSpec((1, XA_HEAD_DIM), lambda bi, i: (0, 0)),
                  pl.BlockSpec((1, XA_HEAD_DIM), lambda bi, i: (0, 0)),
                  pl.BlockSpec((d, d), lambda bi, i: (0, 0))],
        out_specs=pl.BlockSpec((1, ts, d), lambda bi, i: (bi, i, 0)),
        compiler_params=_cparams("parallel", "parallel"),
        name="memory_cross_attention",
    )(h, kv, gain.reshape(1, d), wq, q_gain.reshape(1, -1), k_gain.reshape(1, -1), wo)


def _swiglu_body(h_ref, g_ref, wg_ref, wu_ref, wd_ref, o_ref, hn_ref, acc_ref):
    j = pl.program_id(1)

    @pl.when(j == 0)
    def _():
        hn_ref[...] = _rms(h_ref[...], g_ref[...]).astype(BF16)
        acc_ref[...] = h_ref[...]

    hn = hn_ref[...]
    gate = jnp.dot(hn, wg_ref[...], preferred_element_type=F32)
    up = jnp.dot(hn, wu_ref[...], preferred_element_type=F32)
    act = (_silu(gate) * up).astype(BF16)
    acc_ref[...] += jnp.dot(act, wd_ref[...], preferred_element_type=F32)

    @pl.when(j == pl.num_programs(1) - 1)
    def _():
        o_ref[...] = acc_ref[...]


def _swiglu(h, gain, wg, wu, wd, tm, tf):
    n, d = h.shape
    f = wg.shape[1]
    return pl.pallas_call(
        _swiglu_body,
        out_shape=jax.ShapeDtypeStruct((n, d), F32),
        grid=(n // tm, f // tf),
        in_specs=[pl.BlockSpec((tm, d), lambda i, j: (i, 0)),
                  pl.BlockSpec((1, d), lambda i, j: (0, 0)),
                  pl.BlockSpec((d, tf), lambda i, j: (0, j)),
                  pl.BlockSpec((d, tf), lambda i, j: (0, j)),
                  pl.BlockSpec((tf, d), lambda i, j: (j, 0))],
        out_specs=pl.BlockSpec((tm, d), lambda i, j: (i, 0)),
        scratch_shapes=[pltpu.VMEM((tm, d), BF16), pltpu.VMEM((tm, d), F32)],
        compiler_params=_cparams("parallel", "arbitrary"),
        name="swiglu",
    )(h, gain.reshape(1, d), wg, wu, wd)


def _ssd_body(zx_ref, cw_ref, cb_ref, dtb_ref, alog_ref, dskip_ref, ng_ref, o_ref,
              prev_ref, state_ref):
    c = pl.program_id(1)
    L = SSD_CHUNK
    n_state = SSD_STATE
    pair_w = 2 * SSD_HEAD_DIM
    assert pair_w == LANES and n_state == LANES and L == LANES

    @pl.when(c == 0)
    def _():
        prev_ref[...] = jnp.zeros_like(prev_ref)
        state_ref[...] = jnp.zeros_like(state_ref)

    blk = zx_ref[0]
    z = blk[:, :SSD_D_INNER]
    xr = blk[:, SSD_D_INNER:SSD_D_INNER + SSD_CONV_DIM]
    dtr = blk[:, SSD_D_INNER + SSD_CONV_DIM:SSD_D_INNER + SSD_CONV_DIM + LANES]

    prev = prev_ref[...]
    row = _iota((L, 1), 0)
    conv = xr * cw_ref[SSD_CONV - 1:SSD_CONV, :]
    for k in range(1, SSD_CONV):
        shifted = jnp.where(row < k, pltpu.roll(prev, k, 0), pltpu.roll(xr, k, 0))
        conv = conv + shifted * cw_ref[SSD_CONV - 1 - k:SSD_CONV - k, :]
    prev_ref[...] = xr
    xa = _silu(conv + cb_ref[...])
    xs = xa[:, :SSD_D_INNER]
    bm = xa[:, SSD_D_INNER:SSD_D_INNER + SSD_GROUPS * n_state]
    cm = xa[:, SSD_D_INNER + SSD_GROUPS * n_state:]

    dtx = dtr + dtb_ref[...]
    dt = jnp.maximum(dtx, 0.0) + jnp.log(1.0 + jnp.exp(-jnp.abs(dtx)))
    da = dt * (-jnp.exp(alog_ref[...]))
    tri = jnp.where(_iota((L, L), 1) <= _iota((L, L), 0), 1.0, 0.0)
    acum = jnp.dot(tri, da, precision=HIGHEST, preferred_element_type=F32)
    acum_t = acum.T
    dt_t = dt.T
    to_end_t = jnp.exp(acum_t[:, L - 1:L] - acum_t)
    causal = _iota((L, L), 1) <= _iota((L, L), 0)
    lane = _iota((1, LANES), 1)
    low = lane < SSD_HEAD_DIM
    nt = (((1,), (1,)), ((), ()))

    heads_per_group = SSD_HEADS // SSD_GROUPS
    y_pairs = []
    for g in range(SSD_GROUPS):
        bg = bm[:, g * n_state:(g + 1) * n_state]
        cg = cm[:, g * n_state:(g + 1) * n_state].astype(BF16)
        bg_t = bg.T
        cb = lax.dot_general(cg, bg.astype(BF16), nt, preferred_element_type=F32)
        for pp in range(heads_per_group // 2):
            pidx = g * (heads_per_group // 2) + pp
            cols = slice(pidx * pair_w, (pidx + 1) * pair_w)
            x_pair = xs[:, cols].astype(BF16)
            st_pair = state_ref[:, cols]
            cs = jnp.dot(cg, st_pair.astype(BF16), preferred_element_type=F32)
            ys, upds, lasts = [], [], []
            for hh in range(2):
                h = 2 * pidx + hh
                bc = jnp.broadcast_to(acum[:, h:h + 1], (L, L))
                seg = bc - acum_t[h:h + 1, :]
                dec = jnp.exp(jnp.where(causal, seg, -jnp.inf))
                mm = (cb * dec * dt_t[h:h + 1, :]).astype(BF16)
                ebc = jnp.exp(bc)
                ys.append(jnp.dot(mm, x_pair, preferred_element_type=F32) + ebc * cs)
                wrow = dt_t[h:h + 1, :] * to_end_t[h:h + 1, :]
                upds.append(jnp.dot((bg_t * wrow).astype(BF16), x_pair,
                                    preferred_element_type=F32))
                lasts.append(ebc[L - 1:L, :])
            y_pairs.append(jnp.where(low, ys[0], ys[1]))
            state_ref[:, cols] = (st_pair * jnp.where(low, lasts[0], lasts[1])
                                  + jnp.where(low, upds[0], upds[1]))

    y = jnp.concatenate(y_pairs, axis=1)
    y = (y + dskip_ref[...] * xs) * _silu(z)
    gw = SSD_D_INNER // SSD_GROUPS
    outs = []
    for g in range(SSD_GROUPS):
        cols = slice(g * gw, (g + 1) * gw)
        outs.append(_rms(y[:, cols], ng_ref[:, cols]))
    o_ref[0] = jnp.concatenate(outs, axis=1)


def _ssd_mixer(zx, conv_w, conv_b, dt_bias, a_log, d_skip, norm_g):
    b, s, f = zx.shape
    pad = LANES - SSD_HEADS
    dtb = jnp.pad(dt_bias, (0, pad)).reshape(1, LANES)
    alog = jnp.pad(a_log, (0, pad)).reshape(1, LANES)
    dskip = jnp.repeat(d_skip, SSD_HEAD_DIM).reshape(1, SSD_D_INNER)
    full = lambda shape: pl.BlockSpec(shape, lambda bi, c: (0,) * len(shape))
    return pl.pallas_call(
        _ssd_body,
        out_shape=jax.ShapeDtypeStruct((b, s, SSD_D_INNER), F32),
        grid=(b, s // SSD_CHUNK),
        in_specs=[pl.BlockSpec((1, SSD_CHUNK, f), lambda bi, c: (bi, c, 0)),
                  full((SSD_CONV, SSD_CONV_DIM)), full((1, SSD_CONV_DIM)),
                  full((1, LANES)), full((1, LANES)),
                  full((1, SSD_D_INNER)), full((1, SSD_D_INNER))],
        out_specs=pl.BlockSpec((1, SSD_CHUNK, SSD_D_INNER), lambda bi, c: (bi, c, 0)),
        scratch_shapes=[pltpu.VMEM((SSD_CHUNK, SSD_CONV_DIM), F32),
                        pltpu.VMEM((SSD_STATE, SSD_D_INNER), F32)],
        compiler_params=_cparams("parallel", "arbitrary"),
        name="ssd_mixer",
    )(zx, conv_w, conv_b.reshape(1, -1), dtb, alog, dskip, norm_g.reshape(1, -1))


def _router_body(h_ref, g_ref, wr_ref, hn_ref, lg_ref):
    hn = _rms(h_ref[...], g_ref[...])
    hn_ref[...] = hn
    lg_ref[...] = jnp.dot(hn, wr_ref[...], precision=HIGHEST, preferred_element_type=F32)


def _router(h, gain, w_router_pad, tm):
    n, d = h.shape
    return pl.pallas_call(
        _router_body,
        out_shape=[jax.ShapeDtypeStruct((n, d), F32), jax.ShapeDtypeStruct((n, LANES), F32)],
        grid=(n // tm,),
        in_specs=[pl.BlockSpec((tm, d), lambda i: (i, 0)),
                  pl.BlockSpec((1, d), lambda i: (0, 0)),
                  pl.BlockSpec((d, LANES), lambda i: (0, 0))],
        out_specs=[pl.BlockSpec((tm, d), lambda i: (i, 0)),
                   pl.BlockSpec((tm, LANES), lambda i: (i, 0))],
        compiler_params=_cparams("parallel"),
        name="moe_router",
    )(h, gain.reshape(1, d), w_router_pad)


def _row_copy(src_hbm, dst_vmem, sem, idx, r):
    return pltpu.make_async_copy(src_hbm.at[idx],
                                 dst_vmem.at[pl.ds(pl.multiple_of(r * SUBLANES, SUBLANES), SUBLANES), :],
                                 sem)


def _gather_rows(idx_ref, src_hbm, dst_vmem, sem, n_rows):
    def start(r, carry):
        _row_copy(src_hbm, dst_vmem, sem, idx_ref[0, 0, r], r).start()
        return carry

    def wait(r, carry):
        _row_copy(src_hbm, dst_vmem, sem, 0, r).wait()
        return carry

    lax.fori_loop(0, n_rows, start, 0)
    lax.fori_loop(0, n_rows, wait, 0)


def _rows_to_matrix(buf_ref, first_row, n_rows, d):
    chunks = [buf_ref[pl.ds(first_row * SUBLANES + j, n_rows, stride=SUBLANES), :]
              for j in range(d // LANES)]
    return jnp.concatenate(chunks, axis=1)


def _expert_body(blk_e_ref, n_valid_ref, tok_ref, hn_hbm, gate_ref, wg_ref, wu_ref, wd_ref,
                 o_ref, x_ref, sem):
    i = pl.program_id(0)
    rows, d = o_ref.shape

    @pl.when(n_valid_ref[i] > 0)
    def _():
        _gather_rows(tok_ref, hn_hbm, x_ref, sem, rows)
        x = _rows_to_matrix(x_ref, 0, rows, d).astype(BF16)
        f = wg_ref.shape[2]
        acc = jnp.zeros((rows, d), F32)
        for c0 in range(0, f, MOE_FF_CHUNK):
            cols = slice(c0, c0 + MOE_FF_CHUNK)
            gate = jnp.dot(x, wg_ref[0, :, cols], preferred_element_type=F32)
            up = jnp.dot(x, wu_ref[0, :, cols], preferred_element_type=F32)
            act = (_silu(gate) * up).astype(BF16)
            acc = acc + jnp.dot(act, wd_ref[0, cols, :], preferred_element_type=F32)
        o_ref[...] = acc * gate_ref[...]

    @pl.when(n_valid_ref[i] == 0)
    def _():
        o_ref[...] = jnp.zeros_like(o_ref)


def _expert_ffn(blk_e, n_valid, row_tok, hn_rows, row_gate, wg, wu, wd):
    n_blocks = blk_e.shape[0]
    d = wg.shape[1]
    f = wg.shape[2]
    rows = MOE_ROWS
    grid_spec = pltpu.PrefetchScalarGridSpec(
        num_scalar_prefetch=2,
        grid=(n_blocks,),
        in_specs=[pl.BlockSpec((1, 1, rows), lambda i, be, nv: (i, 0, 0), memory_space=pltpu.SMEM),
                  pl.BlockSpec(memory_space=pl.ANY),
                  pl.BlockSpec((rows, 1), lambda i, be, nv: (i, 0)),
                  pl.BlockSpec((1, d, f), lambda i, be, nv: (be[i], 0, 0)),
                  pl.BlockSpec((1, d, f), lambda i, be, nv: (be[i], 0, 0)),
                  pl.BlockSpec((1, f, d), lambda i, be, nv: (be[i], 0, 0))],
        out_specs=pl.BlockSpec((rows, d), lambda i, be, nv: (i, 0)),
        scratch_shapes=[pltpu.VMEM((rows * SUBLANES, LANES), F32), pltpu.SemaphoreType.DMA(())],
    )
    return pl.pallas_call(
        _expert_body,
        out_shape=jax.ShapeDtypeStruct((n_blocks * rows, d), F32),
        grid_spec=grid_spec,
        compiler_params=_cparams("arbitrary"),
        name="moe_expert_ffn",
    )(blk_e, n_valid, row_tok.reshape(n_blocks, 1, rows), hn_rows, row_gate.reshape(-1, 1),
      wg, wu, wd)


def _combine_body(pos_ref, h_ref, y_hbm, o_ref, y_ref, sem):
    tc, d = o_ref.shape
    _gather_rows(pos_ref, y_hbm, y_ref, sem, TOP_K * tc)
    acc = h_ref[...]
    for k in range(TOP_K):
        acc = acc + _rows_to_matrix(y_ref, k * tc, tc, d)
    o_ref[...] = acc


def _moe_combine(h, pos, y_rows):
    n, d = h.shape
    tc = COMBINE_TOKENS
    n_tiles = n // tc
    pos_tiles = pos.reshape(n_tiles, tc, TOP_K).transpose(0, 2, 1).reshape(n_tiles, 1, TOP_K * tc)
    return pl.pallas_call(
        _combine_body,
        out_shape=jax.ShapeDtypeStruct((n, d), F32),
        grid=(n_tiles,),
        in_specs=[pl.BlockSpec((1, 1, TOP_K * tc), lambda i: (i, 0, 0), memory_space=pltpu.SMEM),
                  pl.BlockSpec((tc, d), lambda i: (i, 0)),
                  pl.BlockSpec(memory_space=pl.ANY)],
        out_specs=pl.BlockSpec((tc, d), lambda i: (i, 0)),
        scratch_shapes=[pltpu.VMEM((TOP_K * tc * SUBLANES, LANES), F32),
                        pltpu.SemaphoreType.DMA(())],
        compiler_params=_cparams("arbitrary"),
        name="moe_combine",
    )(pos_tiles, h, y_rows)


def _moe_layer(h, gain, w_router, wg, wu, wd, tm):
    n, d = h.shape
    e = w_router.shape[1]
    rows = MOE_ROWS
    hn, logits = _router(h, gain, jnp.pad(w_router, ((0, 0), (0, LANES - e))), tm)
    top_logit, top_e = lax.top_k(logits[:, :e], TOP_K)
    gates = jax.nn.softmax(top_logit, axis=-1)

    e_flat = top_e.reshape(-1).astype(jnp.int32)
    order = jnp.argsort(e_flat, stable=True).astype(jnp.int32)
    e_sorted = e_flat[order]
    counts = jnp.zeros((e,), jnp.int32).at[e_flat].add(1)
    start = jnp.cumsum(counts) - counts
    padded = (counts + rows - 1) // rows * rows
    pad_end = jnp.cumsum(padded)
    pad_start = pad_end - padded
    dest = pad_start[e_sorted] + jnp.arange(n * TOP_K, dtype=jnp.int32) - start[e_sorted]
    n_blocks = n * TOP_K // rows + e
    n_rows = n_blocks * rows
    row_tok = jnp.zeros((n_rows,), jnp.int32).at[dest].set(order // TOP_K)
    row_gate = jnp.zeros((n_rows,), F32).at[dest].set(gates.reshape(-1)[order])
    pos = jnp.zeros((n * TOP_K,), jnp.int32).at[order].set(dest)
    blk_start = jnp.arange(n_blocks, dtype=jnp.int32) * rows
    blk_e = jnp.minimum(jnp.sum(blk_start[:, None] >= pad_end[None, :], axis=1),
                        e - 1).astype(jnp.int32)
    n_valid = jnp.clip(pad_start[blk_e] + counts[blk_e] - blk_start, 0, rows).astype(jnp.int32)

    hn_rows = hn.reshape(n, SUBLANES, LANES)
    y = _expert_ffn(blk_e, n_valid, row_tok, hn_rows, row_gate, wg, wu, wd)
    return _moe_combine(h, pos, y.reshape(n_rows, SUBLANES, LANES))


def _rope_tables(positions):
    half = ATT_HEAD_DIM // 2
    inv_freq = ROPE_THETA ** (-jnp.arange(half, dtype=F32) / half)
    ang = positions.astype(F32)[:, None] * inv_freq[None, :]
    cos = jnp.cos(ang)
    sin = jnp.sin(ang)
    reps = LANES // ATT_HEAD_DIM
    return (jnp.tile(cos, (1, 2 * reps)), jnp.tile(jnp.concatenate([-sin, sin], axis=1), (1, reps)))


def kernel(x, mem, positions, mix_norm, xa_norm, mem_norm, ffn_norm, xa_wq, xa_wkv, xa_q_norm, xa_k_norm, xa_wo, hy_w_in, hy_q_norm, hy_k_norm, pool_w, pool_scale, hy_w_out, ffn_w_gate, ffn_w_up, ffn_w_down, ssd_w_in, ssd_conv_w, ssd_conv_b, ssd_dt_bias, ssd_a_log, ssd_d, ssd_norm, ssd_w_out, moe_router, moe_w_gate, moe_w_up, moe_w_down):
    b, s, d = x.shape
    n = b * s
    n_mem = mem.shape[1]
    depth = mix_norm.shape[0]
    tm = min(512, n)
    ts = min(512, s)
    bf = lambda w: w.astype(BF16)
    cos, sin = _rope_tables(positions)
    reps = LANES // ATT_HEAD_DIM

    h = x.reshape(n, d)
    mem2 = mem.reshape(b * n_mem, d)
    for layer in range(depth):
        j = layer // 2
        if layer % 2 == 0:
            proj = _norm_matmul(h, mix_norm[layer], bf(hy_w_in[j]), tm, 1024).reshape(b, s, -1)
            qn, kn, kmean = _qk_prep(proj, cos, sin,
                                     jnp.tile(hy_q_norm[j], reps).reshape(1, LANES),
                                     jnp.tile(hy_k_norm[j], reps).reshape(1, LANES))
            att = _moba_attention(qn, kn, proj, kmean)
            pooled = _multiscale_pool(proj, bf(pool_w[j]), pool_scale[j], ts)
            w_out = bf(hy_w_out[j])
            h = _matmul_residual(h, [(att.reshape(n, -1), w_out[:ATT_WIDTH]),
                                     (pooled.reshape(n, -1), w_out[ATT_WIDTH:])], tm)
        else:
            w_in = ssd_w_in[j]
            w_cat = jnp.pad(w_in, ((0, 0), (0, SSD_DT_PAD - SSD_HEADS)))
            zx = _norm_matmul(h, mix_norm[layer], bf(w_cat), tm, SSD_PROJ_TILE).reshape(b, s, -1)
            y = _ssd_mixer(zx, ssd_conv_w[j], ssd_conv_b[j], ssd_dt_bias[j], ssd_a_log[j],
                           ssd_d[j], ssd_norm[j])
            h = _matmul_residual(h, [(y.reshape(n, -1), bf(ssd_w_out[j]))], tm)

        kv = _norm_matmul(mem2, mem_norm[layer], bf(xa_wkv[layer]), min(512, b * n_mem), 1024)
        h = _cross_attention(h.reshape(b, s, d), kv.reshape(b, n_mem, 2 * d), xa_norm[layer],
                             bf(xa_wq[layer]), xa_q_norm[layer], xa_k_norm[layer],
                             bf(xa_wo[layer]), ts).reshape(n, d)

        if layer % 2 == 0:
            h = _swiglu(h, ffn_norm[layer], bf(ffn_w_gate[j]), bf(ffn_w_up[j]),
                        bf(ffn_w_down[j]), tm, 1408)
        else:
            h = _moe_layer(h, ffn_norm[layer], moe_router[j], bf(moe_w_gate[j]),
                           bf(moe_w_up[j]), bf(moe_w_down[j]), tm)
    return h.reshape(b, s, d)
```

```python
import functools
import math

import jax
import jax.numpy as jnp
from jax import lax
from jax.experimental import pallas as pl
from jax.experimental.pallas import tpu as pltpu

F32 = jnp.float32
BF16 = jnp.bfloat16
HIGHEST = lax.Precision.HIGHEST

EPS = 1e-6
ROPE_THETA = 10000.0
LANES = 128
SUBLANES = 8
VMEM_LIMIT_BYTES = 56 * 1024 * 1024

ATT_HEADS = 8
ATT_HEAD_DIM = 64
ATT_WIDTH = ATT_HEADS * ATT_HEAD_DIM
POOL_WINDOWS = (2, 4, 8, 16)
POOL_GROUP_DIM = 128
POOL_HALO = 16
MOBA_BLOCK = 256
MOBA_TOPK = 3
MOBA_QUERY_TILE = 128
NEG_BIG = -1e30

SSD_D_INNER = 2048
SSD_HEAD_DIM = 64
SSD_HEADS = 32
SSD_GROUPS = 4
SSD_STATE = 128
SSD_CONV = 4
SSD_CHUNK = 128
SSD_CONV_DIM = SSD_D_INNER + 2 * SSD_GROUPS * SSD_STATE
SSD_DT_PAD = 256
SSD_PROJ_TILE = 1792

XA_HEADS = 4
XA_HEAD_DIM = 256

N_EXPERTS = 8
TOP_K = 2
MOE_ROWS = 512
MOE_FF_CHUNK = 896
COMBINE_TOKENS = 256


def _cparams(*sem):
    return pltpu.CompilerParams(dimension_semantics=sem, vmem_limit_bytes=VMEM_LIMIT_BYTES)


def _rms(x, g):
    return x * lax.rsqrt(jnp.mean(x * x, axis=-1, keepdims=True) + EPS) * g


def _silu(x):
    return x / (1.0 + jnp.exp(-x))


def _iota(shape, dim):
    return lax.broadcasted_iota(jnp.int32, shape, dim)


def _norm_matmul_body(x_ref, g_ref, w_ref, o_ref, xn_ref):
    @pl.when(pl.program_id(1) == 0)
    def _():
        xn_ref[...] = _rms(x_ref[...], g_ref[...]).astype(BF16)

    o_ref[...] = jnp.dot(xn_ref[...], w_ref[...], preferred_element_type=F32)


def _norm_matmul(x, gain, w, tm, tn):
    n, d = x.shape
    f = w.shape[1]
    return pl.pallas_call(
        _norm_matmul_body,
        out_shape=jax.ShapeDtypeStruct((n, f), F32),
        grid=(n // tm, f // tn),
        in_specs=[pl.BlockSpec((tm, d), lambda i, j: (i, 0)),
                  pl.BlockSpec((1, d), lambda i, j: (0, 0)),
                  pl.BlockSpec((d, tn), lambda i, j: (0, j))],
        out_specs=pl.BlockSpec((tm, tn), lambda i, j: (i, j)),
        scratch_shapes=[pltpu.VMEM((tm, d), BF16)],
        compiler_params=_cparams("parallel", "arbitrary"),
        name="norm_matmul",
    )(x, gain.reshape(1, d), w)


def _matmul_residual_body(n_pairs, res_ref, *refs):
    o_ref = refs[-1]
    acc = res_ref[...]
    for p in range(n_pairs):
        acc = acc + jnp.dot(refs[2 * p][...].astype(BF16), refs[2 * p + 1][...],
                            preferred_element_type=F32)
    o_ref[...] = acc


def _matmul_residual(res, pairs, tm):
    n, d = res.shape
    in_specs = [pl.BlockSpec((tm, d), lambda i: (i, 0))]
    args = [res]
    for a, w in pairs:
        in_specs.append(pl.BlockSpec((tm, a.shape[1]), lambda i: (i, 0)))
        in_specs.append(pl.BlockSpec(w.shape, lambda i: (0, 0)))
        args += [a, w]
    return pl.pallas_call(
        functools.partial(_matmul_residual_body, len(pairs)),
        out_shape=jax.ShapeDtypeStruct((n, d), F32),
        grid=(n // tm,),
        in_specs=in_specs,
        out_specs=pl.BlockSpec((tm, d), lambda i: (i, 0)),
        compiler_params=_cparams("parallel"),
        name="matmul_residual",
    )(*args)


def _qkv_prep_body(q_ref, k_ref, v_ref, cos_ref, sin_ref, qg_ref, kg_ref,
                   qo_ref, ko_ref, vt_ref, km_ref):
    lane = _iota((1, LANES), 1)
    first_half = (lane % ATT_HEAD_DIM) < (ATT_HEAD_DIM // 2)
    r = _iota((LANES, LANES), 0) // ATT_HEAD_DIM
    c = _iota((LANES, LANES), 1) // ATT_HEAD_DIM
    head_mean = jnp.where(r == c, 1.0 / ATT_HEAD_DIM, 0.0).astype(F32)
    cos = cos_ref[...]
    sin = sin_ref[...]

    def prep(x, g):
        ms = jnp.dot(x * x, head_mean, precision=HIGHEST, preferred_element_type=F32)
        xn = x * lax.rsqrt(ms + EPS) * g
        half = ATT_HEAD_DIM // 2
        partner = jnp.where(first_half, pltpu.roll(xn, LANES - half, 1), pltpu.roll(xn, half, 1))
        return xn * cos + partner * sin

    qo_ref[0] = prep(q_ref[0], qg_ref[...])
    kk = prep(k_ref[0], kg_ref[...])
    ko_ref[0] = kk.astype(BF16)
    km_ref[0, 0] = jnp.mean(kk, axis=0, keepdims=True)
    vt_ref[0, 0] = v_ref[0].T.astype(BF16)


def _qkv_prep(proj, cos, sin, qg, kg):
    b, s, _ = proj.shape
    nb = s // MOBA_BLOCK
    n_pairs = ATT_WIDTH // LANES
    blk = (1, MOBA_BLOCK, LANES)
    qn, kn, vt, km = pl.pallas_call(
        _qkv_prep_body,
        out_shape=[jax.ShapeDtypeStruct((b, s, ATT_WIDTH), F32),
                   jax.ShapeDtypeStruct((b, s, ATT_WIDTH), BF16),
                   jax.ShapeDtypeStruct((b, nb, ATT_WIDTH, MOBA_BLOCK), BF16),
                   jax.ShapeDtypeStruct((b, nb, 1, ATT_WIDTH), F32)],
        grid=(b, nb, n_pairs),
        in_specs=[pl.BlockSpec(blk, lambda bi, i, p: (bi, i, p)),
                  pl.BlockSpec(blk, lambda bi, i, p: (bi, i, n_pairs + p)),
                  pl.BlockSpec(blk, lambda bi, i, p: (bi, i, 2 * n_pairs + p)),
                  pl.BlockSpec((MOBA_BLOCK, LANES), lambda bi, i, p: (i, 0)),
                  pl.BlockSpec((MOBA_BLOCK, LANES), lambda bi, i, p: (i, 0)),
                  pl.BlockSpec((1, LANES), lambda bi, i, p: (0, 0)),
                  pl.BlockSpec((1, LANES), lambda bi, i, p: (0, 0))],
        out_specs=[pl.BlockSpec(blk, lambda bi, i, p: (bi, i, p)),
                   pl.BlockSpec(blk, lambda bi, i, p: (bi, i, p)),
                   pl.BlockSpec((1, 1, LANES, MOBA_BLOCK), lambda bi, i, p: (bi, i, p, 0)),
                   pl.BlockSpec((1, 1, 1, LANES), lambda bi, i, p: (bi, i, 0, p))],
        compiler_params=_cparams("parallel", "parallel", "parallel"),
        name="moba_qkv_prep",
    )(proj, proj, proj, cos, sin, qg, kg)
    return qn, kn, vt, km.reshape(b, nb, ATT_WIDTH)


def _moba_body(q_ref, k_ref, vt_ref, km_ref, o_ref, bias_ref, s_ref, p_ref):
    i = pl.program_id(2)
    bs = MOBA_BLOCK
    qt = MOBA_QUERY_TILE
    q = q_ref[0]
    km = km_ref[0]
    nb = km.shape[0]
    scale = ATT_HEAD_DIM ** -0.5
    lane = _iota((1, LANES), 1)
    blk = _iota((nb, 1), 0)
    nt = (((1,), (1,)), ((), ()))

    streams = []
    for hh in range(2):
        qh = jnp.where((lane // ATT_HEAD_DIM) == hh, q, 0.0)
        gate = lax.dot_general(km, qh, nt, precision=HIGHEST, preferred_element_type=F32)
        gate = jnp.where(blk < i, gate, -jnp.inf)
        cnt = jnp.zeros((nb, bs), F32)
        for jp in range(nb):
            gj = gate[jp:jp + 1, :]
            cnt = cnt + jnp.where(jp < blk, jnp.where(gj >= gate, 1.0, 0.0),
                                  jnp.where(gj > gate, 1.0, 0.0))
        keep = jnp.logical_and(blk < i, cnt < MOBA_TOPK)
        sel_bias = jnp.where(keep, 0.0, NEG_BIG)
        for jp in range(nb):
            bias_ref[hh, jp] = sel_bias[jp:jp + 1, :]
        qb_all = (qh * scale).astype(BF16)
        vrows = slice(hh * ATT_HEAD_DIM, (hh + 1) * ATT_HEAD_DIM)
        for qs in range(0, bs, qt):
            streams.append((hh, qs, qb_all[qs:qs + qt, :], vrows))
    n_streams = len(streams)

    def score_tiles(j):
        k_j = k_ref[0, pl.ds(pl.multiple_of(j * bs, bs), bs), :]
        return tuple(lax.dot_general(k_j, qb, nt, preferred_element_type=F32)
                     for (_, _, qb, _) in streams)

    def value_products(j, probs):
        return [jnp.dot(vt_ref[0, j, vrows, :], probs[si], preferred_element_type=F32)
                for si, (_, _, _, vrows) in enumerate(streams)]

    ms, ls, accs = [], [], []
    for si, s in enumerate(score_tiles(i)):
        qs = streams[si][1]
        causal = _iota((bs, qt), 0) <= qs + _iota((bs, qt), 1)
        s = jnp.where(causal, s, NEG_BIG)
        m = jnp.max(s, axis=0, keepdims=True)
        p = jnp.exp(s - m)
        ms.append(m)
        ls.append(jnp.sum(p, axis=0, keepdims=True))
        accs.append(jnp.zeros((ATT_HEAD_DIM, qt), F32))
        p_ref[1, si] = p.astype(BF16)
    for si, s in enumerate(score_tiles(0)):
        s_ref[0, si] = s

    def handle(j, slot, state):
        ms, ls, accs = state
        for si, s in enumerate(score_tiles(jnp.minimum(j + 1, nb - 1))):
            s_ref[1 - slot, si] = s
        owed = value_products(jnp.where(j == 0, i, j - 1),
                              [p_ref[1 - slot, si] for si in range(n_streams)])
        ms_o, ls_o, accs_o = [], [], []
        for si, (hh, qs, _, _) in enumerate(streams):
            s = s_ref[slot, si] + bias_ref[hh, j, :, qs:qs + qt]
            m_new = jnp.maximum(ms[si], jnp.max(s, axis=0, keepdims=True))
            alpha = jnp.exp(ms[si] - m_new)
            p = jnp.exp(s - m_new)
            ms_o.append(m_new)
            ls_o.append(alpha * ls[si] + jnp.sum(p, axis=0, keepdims=True))
            accs_o.append(alpha * (accs[si] + owed[si]))
            p_ref[slot, si] = p.astype(BF16)
        return tuple(ms_o), tuple(ls_o), tuple(accs_o)

    def two_blocks(t, state):
        return handle(2 * t + 1, 1, handle(2 * t, 0, state))

    trips = (i + 1) // 2
    ms, ls, accs = lax.fori_loop(0, trips, two_blocks, (tuple(ms), tuple(ls), tuple(accs)))
    owed = value_products(jnp.where(trips == 0, i, 2 * trips - 1),
                          [p_ref[1, si] for si in range(n_streams)])
    outs = [(accs[si] + owed[si]) / ls[si] for si in range(n_streams)]
    per_head = bs // qt
    out_rows = [jnp.concatenate(outs[hh * per_head:(hh + 1) * per_head], axis=1) for hh in range(2)]
    o_ref[0] = jnp.concatenate(out_rows, axis=0).T


def _moba_attention(qn, kn, vt, kmean):
    b, s, _ = qn.shape
    nb = s // MOBA_BLOCK
    n_pairs = ATT_WIDTH // LANES
    n_streams = 2 * MOBA_BLOCK // MOBA_QUERY_TILE
    return pl.pallas_call(
        _moba_body,
        out_shape=jax.ShapeDtypeStruct((b, s, ATT_WIDTH), F32),
        grid=(b, n_pairs, nb),
        in_specs=[pl.BlockSpec((1, MOBA_BLOCK, LANES), lambda bi, p, i: (bi, i, p)),
                  pl.BlockSpec((1, s, LANES), lambda bi, p, i: (bi, 0, p)),
                  pl.BlockSpec((1, nb, LANES, MOBA_BLOCK), lambda bi, p, i: (bi, 0, p, 0)),
                  pl.BlockSpec((1, nb, LANES), lambda bi, p, i: (bi, 0, p))],
        out_specs=pl.BlockSpec((1, MOBA_BLOCK, LANES), lambda bi, p, i: (bi, i, p)),
        scratch_shapes=[pltpu.VMEM((2, nb, 1, MOBA_BLOCK), F32),
                        pltpu.VMEM((2, n_streams, MOBA_BLOCK, MOBA_QUERY_TILE), F32),
                        pltpu.VMEM((2, n_streams, MOBA_BLOCK, MOBA_QUERY_TILE), BF16)],
        compiler_params=_cparams("parallel", "parallel", "arbitrary"),
        name="moba_attention",
    )(qn, kn, vt, kmean)


def _pool_body(ts, u_ref, halo_ref, w_ref, sc_ref, o_ref, ext_ref):
    i = pl.program_id(1)
    u = u_ref[0]
    ext_ref[0:POOL_HALO, :] = jnp.where(i > 0, halo_ref[0], 0.0)
    ext_ref[POOL_HALO:POOL_HALO + ts, :] = u
    t = i * ts + _iota((ts, 1), 0)
    outs = []
    for g, win in enumerate(POOL_WINDOWS):
        cols = slice(g * POOL_GROUP_DIM, (g + 1) * POOL_GROUP_DIM)
        ug = u[:, cols]
        acc = ug
        for k in range(1, win):
            acc = acc + ext_ref[POOL_HALO - k:POOL_HALO - k + ts, cols]
        cnt = jnp.minimum(t + 1, win).astype(F32)
        pooled = acc / cnt - ug
        outs.append(jnp.dot(pooled.astype(BF16), w_ref[g], preferred_element_type=F32))
    o_ref[0] = jnp.concatenate(outs, axis=1) * sc_ref[...]


def _multiscale_pool(proj, w_pool, pool_scale, ts):
    b, s, f = proj.shape
    width = len(POOL_WINDOWS) * POOL_GROUP_DIM
    assert max(POOL_WINDOWS) <= POOL_HALO and f % width == 0
    col = f // width - 1
    halo_per_tile = ts // POOL_HALO
    return pl.pallas_call(
        functools.partial(_pool_body, ts),
        out_shape=jax.ShapeDtypeStruct((b, s, width), F32),
        grid=(b, s // ts),
        in_specs=[pl.BlockSpec((1, ts, width), lambda bi, i: (bi, i, col)),
                  pl.BlockSpec((1, POOL_HALO, width),
                               lambda bi, i: (bi, jnp.maximum(i * halo_per_tile - 1, 0), col)),
                  pl.BlockSpec(w_pool.shape, lambda bi, i: (0, 0, 0)),
                  pl.BlockSpec((1, width), lambda bi, i: (0, 0))],
        out_specs=pl.BlockSpec((1, ts, width), lambda bi, i: (bi, i, 0)),
        scratch_shapes=[pltpu.VMEM((POOL_HALO + ts, width), F32)],
        compiler_params=_cparams("parallel", "parallel"),
        name="multiscale_pool",
    )(proj, proj, w_pool, pool_scale.reshape(1, width))


def _xattn_body(h_ref, kv_ref, g_ref, wq_ref, qg_ref, kg_ref, wo_ref, o_ref):
    h = h_ref[0]
    d = h.shape[-1]
    kv = kv_ref[0]
    hn = _rms(h, g_ref[...]).astype(BF16)
    q = jnp.dot(hn, wq_ref[...], preferred_element_type=F32)
    scale = XA_HEAD_DIM ** -0.5
    nt = (((1,), (1,)), ((), ()))
    outs = []
    for hh in range(XA_HEADS):
        cols = slice(hh * XA_HEAD_DIM, (hh + 1) * XA_HEAD_DIM)
        qh = _rms(q[:, cols], qg_ref[...]).astype(BF16)
        kh = _rms(kv[:, cols], kg_ref[...]).astype(BF16)
        vh = kv[:, d + hh * XA_HEAD_DIM:d + (hh + 1) * XA_HEAD_DIM].astype(BF16)
        s = lax.dot_general(qh, kh, nt, preferred_element_type=F32) * scale
        m = jnp.max(s, axis=-1, keepdims=True)
        p = jnp.exp(s - m)
        p = p / jnp.sum(p, axis=-1, keepdims=True)
        outs.append(jnp.dot(p.astype(BF16), vh, preferred_element_type=F32))
    o = jnp.concatenate(outs, axis=1).astype(BF16)
    o_ref[0] = h + jnp.dot(o, wo_ref[...], preferred_element_type=F32)


def _cross_attention(h, kv, gain, wq, q_gain, k_gain, wo, ts):
    b, s, d = h.shape
    m = kv.shape[1]
    return pl.pallas_call(
        _xattn_body,
        out_shape=jax.ShapeDtypeStruct((b, s, d), F32),
        grid=(b, s // ts),
        in_specs=[pl.BlockSpec((1, ts, d), lambda bi, i: (bi, i, 0)),
                  pl.BlockSpec((1, m, 2 * d), lambda bi, i: (bi, 0, 0)),
                  pl.BlockSpec((1, d), lambda bi, i: (0, 0)),
                  pl.BlockSpec((d, d), lambda bi, i: (0, 0)),
                  pl.BlockSpec((1, XA_HEAD_DIM), lambda bi, i: (0, 0)),
                  pl.BlockSpec((1, XA_HEAD_DIM), lambda bi, i: (0, 0)),
                  pl.BlockSpec((d, d), lambda bi, i: (0, 0))],
        out_specs=pl.BlockSpec((1, ts, d), lambda bi, i: (bi, i, 0)),
        compiler_params=_cparams("parallel", "parallel"),
        name="memory_cross_attention",
    )(h, kv, gain.reshape(1, d), wq, q_gain.reshape(1, -1), k_gain.reshape(1, -1), wo)


def _swiglu_body(h_ref, g_ref, wg_ref, wu_ref, wd_ref, o_ref, hn_ref, acc_ref):
    j = pl.program_id(1)

    @pl.when(j == 0)
    def _():
        hn_ref[...] = _rms(h_ref[...], g_ref[...]).astype(BF16)
        acc_ref[...] = h_ref[...]

    hn = hn_ref[...]
    gate = jnp.dot(hn, wg_ref[...], preferred_element_type=F32)
    up = jnp.dot(hn, wu_ref[...], preferred_element_type=F32)
    act = (_silu(gate) * up).astype(BF16)
    acc_ref[...] += jnp.dot(act, wd_ref[...], preferred_element_type=F32)

    @pl.when(j == pl.num_programs(1) - 1)
    def _():
        o_ref[...] = acc_ref[...]


def _swiglu(h, gain, wg, wu, wd, tm, tf):
    n, d = h.shape
    f = wg.shape[1]
    return pl.pallas_call(
        _swiglu_body,
        out_shape=jax.ShapeDtypeStruct((n, d), F32),
        grid=(n // tm, f // tf),
        in_specs=[pl.BlockSpec((tm, d), lambda i, j: (i, 0)),
                  pl.BlockSpec((1, d), lambda i, j: (0, 0)),
                  pl.BlockSpec((d, tf), lambda i, j: (0, j)),
                  pl.BlockSpec((d, tf), lambda i, j: (0, j)),
                  pl.BlockSpec((tf, d), lambda i, j: (j, 0))],
        out_specs=pl.BlockSpec((tm, d), lambda i, j: (i, 0)),
        scratch_shapes=[pltpu.VMEM((tm, d), BF16), pltpu.VMEM((tm, d), F32)],
        compiler_params=_cparams("parallel", "arbitrary"),
        name="swiglu",
    )(h, gain.reshape(1, d), wg, wu, wd)


def _ssd_body(zx_ref, cw_ref, cb_ref, dtb_ref, alog_ref, dskip_ref, ng_ref, o_ref,
              prev_ref, state_ref):
    c = pl.program_id(1)
    L = SSD_CHUNK
    n_state = SSD_STATE
    pair_w = 2 * SSD_HEAD_DIM
    assert pair_w == LANES and n_state == LANES and L == LANES

    @pl.when(c == 0)
    def _():
        prev_ref[...] = jnp.zeros_like(prev_ref)
        state_ref[...] = jnp.zeros_like(state_ref)

    blk = zx_ref[0]
    z = blk[:, :SSD_D_INNER]
    xr = blk[:, SSD_D_INNER:SSD_D_INNER + SSD_CONV_DIM]
    dtr = blk[:, SSD_D_INNER + SSD_CONV_DIM:SSD_D_INNER + SSD_CONV_DIM + LANES]

    prev = prev_ref[...]
    row = _iota((L, 1), 0)
    conv = xr * cw_ref[SSD_CONV - 1:SSD_CONV, :]
    for k in range(1, SSD_CONV):
        shifted = jnp.where(row < k, pltpu.roll(prev, k, 0), pltpu.roll(xr, k, 0))
        conv = conv + shifted * cw_ref[SSD_CONV - 1 - k:SSD_CONV - k, :]
    prev_ref[...] = xr
    xa = _silu(conv + cb_ref[...])
    xs = xa[:, :SSD_D_INNER]
    bm = xa[:, SSD_D_INNER:SSD_D_INNER + SSD_GROUPS * n_state]
    cm = xa[:, SSD_D_INNER + SSD_GROUPS * n_state:]

    dtx = dtr + dtb_ref[...]
    dt = jnp.maximum(dtx, 0.0) + jnp.log(1.0 + jnp.exp(-jnp.abs(dtx)))
    da = dt * (-jnp.exp(alog_ref[...]))
    tri = jnp.where(_iota((L, L), 1) <= _iota((L, L), 0), 1.0, 0.0)
    acum = jnp.dot(tri, da, precision=HIGHEST, preferred_element_type=F32)
    acum_t = acum.T
    dt_t = dt.T
    to_end_t = jnp.exp(acum_t[:, L - 1:L] - acum_t)
    causal = _iota((L, L), 1) <= _iota((L, L), 0)
    lane = _iota((1, LANES), 1)
    low = lane < SSD_HEAD_DIM
    nt = (((1,), (1,)), ((), ()))

    heads_per_group = SSD_HEADS // SSD_GROUPS
    y_pairs = []
    for g in range(SSD_GROUPS):
        bg = bm[:, g * n_state:(g + 1) * n_state]
        cg = cm[:, g * n_state:(g + 1) * n_state].astype(BF16)
        bg_t = bg.T
        cb = lax.dot_general(cg, bg.astype(BF16), nt, preferred_element_type=F32)
        for pp in range(heads_per_group // 2):
            pidx = g * (heads_per_group // 2) + pp
            cols = slice(pidx * pair_w, (pidx + 1) * pair_w)
            x_pair = xs[:, cols].astype(BF16)
            st_pair = state_ref[:, cols]
            cs = jnp.dot(cg, st_pair.astype(BF16), preferred_element_type=F32)
            ys, upds, lasts = [], [], []
            for hh in range(2):
                h = 2 * pidx + hh
                bc = jnp.broadcast_to(acum[:, h:h + 1], (L, L))
                seg = bc - acum_t[h:h + 1, :]
                dec = jnp.exp(jnp.where(causal, seg, -jnp.inf))
                mm = (cb * dec * dt_t[h:h + 1, :]).astype(BF16)
                ebc = jnp.exp(bc)
                ys.append(jnp.dot(mm, x_pair, preferred_element_type=F32) + ebc * cs)
                wrow = dt_t[h:h + 1, :] * to_end_t[h:h + 1, :]
                upds.append(jnp.dot((bg_t * wrow).astype(BF16), x_pair,
                                    preferred_element_type=F32))
                lasts.append(ebc[L - 1:L, :])
            y_pairs.append(jnp.where(low, ys[0], ys[1]))
            state_ref[:, cols] = (st_pair * jnp.where(low, lasts[0], lasts[1])
                                  + jnp.where(low, upds[0], upds[1]))

    y = jnp.concatenate(y_pairs, axis=1)
    y = (y + dskip_ref[...] * xs) * _silu(z)
    gw = SSD_D_INNER // SSD_GROUPS
    outs = []
    for g in range(SSD_GROUPS):
        cols = slice(g * gw, (g + 1) * gw)
        outs.append(_rms(y[:, cols], ng_ref[:, cols]))
    o_ref[0] = jnp.concatenate(outs, axis=1)


def _ssd_mixer(zx, conv_w, conv_b, dt_bias, a_log, d_skip, norm_g):
    b, s, f = zx.shape
    pad = LANES - SSD_HEADS
    dtb = jnp.pad(dt_bias, (0, pad)).reshape(1, LANES)
    alog = jnp.pad(a_log, (0, pad)).reshape(1, LANES)
    dskip = jnp.repeat(d_skip, SSD_HEAD_DIM).reshape(1, SSD_D_INNER)
    full = lambda shape: pl.BlockSpec(shape, lambda bi, c: (0,) * len(shape))
    return pl.pallas_call(
        _ssd_body,
        out_shape=jax.ShapeDtypeStruct((b, s, SSD_D_INNER), F32),
        grid=(b, s // SSD_CHUNK),
        in_specs=[pl.BlockSpec((1, SSD_CHUNK, f), lambda bi, c: (bi, c, 0)),
                  full((SSD_CONV, SSD_CONV_DIM)), full((1, SSD_CONV_DIM)),
                  full((1, LANES)), full((1, LANES)),
                  full((1, SSD_D_INNER)), full((1, SSD_D_INNER))],
        out_specs=pl.BlockSpec((1, SSD_CHUNK, SSD_D_INNER), lambda bi, c: (bi, c, 0)),
        scratch_shapes=[pltpu.VMEM((SSD_CHUNK, SSD_CONV_DIM), F32),
                        pltpu.VMEM((SSD_STATE, SSD_D_INNER), F32)],
        compiler_params=_cparams("parallel", "arbitrary"),
        name="ssd_mixer",
    )(zx, conv_w, conv_b.reshape(1, -1), dtb, alog, dskip, norm_g.reshape(1, -1))


def _router_body(h_ref, g_ref, wr_ref, hn_ref, lg_ref):
    hn = _rms(h_ref[...], g_ref[...])
    hn_ref[...] = hn
    lg_ref[...] = jnp.dot(hn, wr_ref[...], precision=HIGHEST, preferred_element_type=F32)


def _router(h, gain, w_router_pad, tm):
    n, d = h.shape
    return pl.pallas_call(
        _router_body,
        out_shape=[jax.ShapeDtypeStruct((n, d), F32), jax.ShapeDtypeStruct((n, LANES), F32)],
        grid=(n // tm,),
        in_specs=[pl.BlockSpec((tm, d), lambda i: (i, 0)),
                  pl.BlockSpec((1, d), lambda i: (0, 0)),
                  pl.BlockSpec((d, LANES), lambda i: (0, 0))],
        out_specs=[pl.BlockSpec((tm, d), lambda i: (i, 0)),
                   pl.BlockSpec((tm, LANES), lambda i: (i, 0))],
        compiler_params=_cparams("parallel"),
        name="moe_router",
    )(h, gain.reshape(1, d), w_router_pad)


def _row_copy(src_hbm, dst_ref, sems, slot, idx, r):
    rows = pl.ds(pl.multiple_of(r * SUBLANES, SUBLANES), SUBLANES)
    return pltpu.make_async_copy(src_hbm.at[idx], dst_ref.at[slot, rows, :], sems.at[slot])


def _start_row_gather(idx_ref, src_hbm, dst_ref, sems, slot, n_rows):
    def body(r, carry):
        _row_copy(src_hbm, dst_ref, sems, slot, idx_ref[0, 0, r], r).start()
        return carry

    lax.fori_loop(0, n_rows, body, 0, unroll=8)


def _wait_row_gather(src_hbm, dst_ref, sems, slot, n_rows):
    def body(r, carry):
        _row_copy(src_hbm, dst_ref, sems, slot, 0, r).wait()
        return carry

    lax.fori_loop(0, n_rows, body, 0, unroll=8)


def _rows_to_matrix(buf_ref, slot, first_row, n_rows, d):
    chunks = [buf_ref[slot, pl.ds(first_row * SUBLANES + j, n_rows, stride=SUBLANES), :]
              for j in range(d // LANES)]
    return jnp.concatenate(chunks, axis=1)


def _expert_body(blk_e_ref, n_valid_ref, tok_ref, tok_next_ref, hn_hbm, gate_ref,
                 wg_ref, wu_ref, wd_ref, o_ref, x_ref, sems):
    i = pl.program_id(0)
    last = pl.num_programs(0) - 1
    rows, d = o_ref.shape
    slot = i % 2

    @pl.when(jnp.logical_and(i == 0, n_valid_ref[0] > 0))
    def _():
        _start_row_gather(tok_ref, hn_hbm, x_ref, sems, 0, rows)

    @pl.when(jnp.logical_and(i < last, n_valid_ref[jnp.minimum(i + 1, last)] > 0))
    def _():
        _start_row_gather(tok_next_ref, hn_hbm, x_ref, sems, 1 - slot, rows)

    @pl.when(n_valid_ref[i] > 0)
    def _():
        _wait_row_gather(hn_hbm, x_ref, sems, slot, rows)
        x = _rows_to_matrix(x_ref, slot, 0, rows, d).astype(BF16)
        f = wg_ref.shape[2]
        acc = jnp.zeros((rows, d), F32)
        for c0 in range(0, f, MOE_FF_CHUNK):
            cols = slice(c0, c0 + MOE_FF_CHUNK)
            gate = jnp.dot(x, wg_ref[0, :, cols], preferred_element_type=F32)
            up = jnp.dot(x, wu_ref[0, :, cols], preferred_element_type=F32)
            act = (_silu(gate) * up).astype(BF16)
            acc = acc + jnp.dot(act, wd_ref[0, cols, :], preferred_element_type=F32)
        o_ref[...] = acc * gate_ref[...]

    @pl.when(n_valid_ref[i] == 0)
    def _():
        o_ref[...] = jnp.zeros_like(o_ref)


def _expert_ffn(blk_e, n_valid, row_tok, hn_rows, row_gate, wg, wu, wd):
    n_blocks = blk_e.shape[0]
    d = wg.shape[1]
    f = wg.shape[2]
    rows = MOE_ROWS
    tok = row_tok.reshape(n_blocks, 1, rows)
    grid_spec = pltpu.PrefetchScalarGridSpec(
        num_scalar_prefetch=2,
        grid=(n_blocks,),
        in_specs=[pl.BlockSpec((1, 1, rows), lambda i, be, nv: (i, 0, 0), memory_space=pltpu.SMEM),
                  pl.BlockSpec((1, 1, rows), lambda i, be, nv: (jnp.minimum(i + 1, n_blocks - 1), 0, 0),
                               memory_space=pltpu.SMEM),
                  pl.BlockSpec(memory_space=pl.ANY),
                  pl.BlockSpec((rows, 1), lambda i, be, nv: (i, 0)),
                  pl.BlockSpec((1, d, f), lambda i, be, nv: (be[i], 0, 0)),
                  pl.BlockSpec((1, d, f), lambda i, be, nv: (be[i], 0, 0)),
                  pl.BlockSpec((1, f, d), lambda i, be, nv: (be[i], 0, 0))],
        out_specs=pl.BlockSpec((rows, d), lambda i, be, nv: (i, 0)),
        scratch_shapes=[pltpu.VMEM((2, rows * SUBLANES, LANES), F32),
                        pltpu.SemaphoreType.DMA((2,))],
    )
    return pl.pallas_call(
        _expert_body,
        out_shape=jax.ShapeDtypeStruct((n_blocks * rows, d), F32),
        grid_spec=grid_spec,
        compiler_params=_cparams("arbitrary"),
        name="moe_expert_ffn",
    )(blk_e, n_valid, tok, tok, hn_rows, row_gate.reshape(-1, 1), wg, wu, wd)


def _combine_body(pos_ref, pos_next_ref, h_ref, y_hbm, o_ref, y_ref, sems):
    i = pl.program_id(0)
    last = pl.num_programs(0) - 1
    tc, d = o_ref.shape
    slot = i % 2

    @pl.when(i == 0)
    def _():
        _start_row_gather(pos_ref, y_hbm, y_ref, sems, 0, TOP_K * tc)

    @pl.when(i < last)
    def _():
        _start_row_gather(pos_next_ref, y_hbm, y_ref, sems, 1 - slot, TOP_K * tc)

    _wait_row_gather(y_hbm, y_ref, sems, slot, TOP_K * tc)
    acc = h_ref[...]
    for k in range(TOP_K):
        acc = acc + _rows_to_matrix(y_ref, slot, k * tc, tc, d)
    o_ref[...] = acc


def _moe_combine(h, pos, y_rows):
    n, d = h.shape
    tc = COMBINE_TOKENS
    n_tiles = n // tc
    pos_tiles = pos.reshape(n_tiles, tc, TOP_K).transpose(0, 2, 1).reshape(n_tiles, 1, TOP_K * tc)
    return pl.pallas_call(
        _combine_body,
        out_shape=jax.ShapeDtypeStruct((n, d), F32),
        grid=(n_tiles,),
        in_specs=[pl.BlockSpec((1, 1, TOP_K * tc), lambda i: (i, 0, 0), memory_space=pltpu.SMEM),
                  pl.BlockSpec((1, 1, TOP_K * tc), lambda i: (jnp.minimum(i + 1, n_tiles - 1), 0, 0),
                               memory_space=pltpu.SMEM),
                  pl.BlockSpec((tc, d), lambda i: (i, 0)),
                  pl.BlockSpec(memory_space=pl.ANY)],
        out_specs=pl.BlockSpec((tc, d), lambda i: (i, 0)),
        scratch_shapes=[pltpu.VMEM((2, TOP_K * tc * SUBLANES, LANES), F32),
                        pltpu.SemaphoreType.DMA((2,))],
        compiler_params=_cparams("arbitrary"),
        name="moe_combine",
    )(pos_tiles, pos_tiles, h, y_rows)


def _moe_layer(h, gain, w_router, wg, wu, wd, tm):
    n, d = h.shape
    e = w_router.shape[1]
    rows = MOE_ROWS
    n_slots = n * TOP_K
    hn, logits = _router(h, gain, jnp.pad(w_router, ((0, 0), (0, LANES - e))), tm)
    top_logit, top_e = lax.top_k(logits[:, :e], TOP_K)
    gates = jax.nn.softmax(top_logit, axis=-1)

    e_flat = top_e.reshape(-1).astype(jnp.int32)
    slot_ids = jnp.arange(n_slots, dtype=jnp.int32)
    e_sorted, order = lax.sort((e_flat, slot_ids), num_keys=1, is_stable=True)
    experts = jnp.arange(e, dtype=jnp.int32)
    sorted_onehot = e_sorted[:, None] == experts[None, :]
    counts = jnp.sum(sorted_onehot, axis=0, dtype=jnp.int32)
    start = jnp.cumsum(counts) - counts
    padded = (counts + rows - 1) // rows * rows
    pad_end = jnp.cumsum(padded)
    pad_start = pad_end - padded
    shift = jnp.sum(jnp.where(sorted_onehot, (pad_start - start)[None, :], 0), axis=1)
    dest = slot_ids + shift
    _, pos = lax.sort((order, dest), num_keys=1)
    n_blocks = n_slots // rows + e
    n_rows = n_blocks * rows
    blk_start = jnp.arange(n_blocks, dtype=jnp.int32) * rows
    blk_e = jnp.minimum(jnp.sum(blk_start[:, None] >= pad_end[None, :], axis=1),
                        e - 1).astype(jnp.int32)
    n_valid = jnp.clip(pad_start[blk_e] + counts[blk_e] - blk_start, 0, rows).astype(jnp.int32)
    row_rank = (jnp.arange(rows, dtype=jnp.int32)[None, :] + (blk_start - pad_start[blk_e])[:, None])
    row_ok = row_rank < counts[blk_e][:, None]
    src = jnp.where(row_ok, start[blk_e][:, None] + row_rank, 0).reshape(-1)
    row_slot = order[src]
    row_tok = jnp.where(row_ok.reshape(-1), row_slot // TOP_K, 0)
    row_gate = jnp.where(row_ok.reshape(-1), gates.reshape(-1)[row_slot], 0.0)

    hn_rows = hn.reshape(n, SUBLANES, LANES)
    y = _expert_ffn(blk_e, n_valid, row_tok, hn_rows, row_gate, wg, wu, wd)
    return _moe_combine(h, pos, y.reshape(n_rows, SUBLANES, LANES))


def _rope_tables(positions):
    half = ATT_HEAD_DIM // 2
    inv_freq = ROPE_THETA ** (-jnp.arange(half, dtype=F32) / half)
    ang = positions.astype(F32)[:, None] * inv_freq[None, :]
    cos = jnp.cos(ang)
    sin = jnp.sin(ang)
    reps = LANES // ATT_HEAD_DIM
    return (jnp.tile(cos, (1, 2 * reps)), jnp.tile(jnp.concatenate([-sin, sin], axis=1), (1, reps)))


def kernel(x, mem, positions, mix_norm, xa_norm, mem_norm, ffn_norm, xa_wq, xa_wkv, xa_q_norm, xa_k_norm, xa_wo, hy_w_in, hy_q_norm, hy_k_norm, pool_w, pool_scale, hy_w_out, ffn_w_gate, ffn_w_up, ffn_w_down, ssd_w_in, ssd_conv_w, ssd_conv_b, ssd_dt_bias, ssd_a_log, ssd_d, ssd_norm, ssd_w_out, moe_router, moe_w_gate, moe_w_up, moe_w_down):
    b, s, d = x.shape
    n = b * s
    n_mem = mem.shape[1]
    depth = mix_norm.shape[0]
    tm = min(512, n)
    ts = min(512, s)
    bf = lambda w: w.astype(BF16)
    cos, sin = _rope_tables(positions)
    reps = LANES // ATT_HEAD_DIM

    h = x.reshape(n, d)
    mem2 = mem.reshape(b * n_mem, d)
    for layer in range(depth):
        j = layer // 2
        if layer % 2 == 0:
            proj = _norm_matmul(h, mix_norm[layer], bf(hy_w_in[j]), min(1024, n), 2048).reshape(b, s, -1)
            qn, kn, vt, kmean = _qkv_prep(proj, cos, sin,
                                          jnp.tile(hy_q_norm[j], reps).reshape(1, LANES),
                                          jnp.tile(hy_k_norm[j], reps).reshape(1, LANES))
            att = _moba_attention(qn, kn, vt, kmean)
            pooled = _multiscale_pool(proj, bf(pool_w[j]), pool_scale[j], ts)
            w_out = bf(hy_w_out[j])
            h = _matmul_residual(h, [(att.reshape(n, -1), w_out[:ATT_WIDTH]),
                                     (pooled.reshape(n, -1), w_out[ATT_WIDTH:])], tm)
        else:
            w_in = ssd_w_in[j]
            w_cat = jnp.pad(w_in, ((0, 0), (0, SSD_DT_PAD - SSD_HEADS)))
            zx = _norm_matmul(h, mix_norm[layer], bf(w_cat), min(1024, n), SSD_PROJ_TILE).reshape(b, s, -1)
            y = _ssd_mixer(zx, ssd_conv_w[j], ssd_conv_b[j], ssd_dt_bias[j], ssd_a_log[j],
                           ssd_d[j], ssd_norm[j])
            h = _matmul_residual(h, [(y.reshape(n, -1), bf(ssd_w_out[j]))], tm)

        kv = _norm_matmul(mem2, mem_norm[layer], bf(xa_wkv[layer]), min(512, b * n_mem), 1024)
        h = _cross_attention(h.reshape(b, s, d), kv.reshape(b, n_mem, 2 * d), xa_norm[layer],
                             bf(xa_wq[layer]), xa_q_norm[layer], xa_k_norm[layer],
                             bf(xa_wo[layer]), ts).reshape(n, d)

        if layer % 2 == 0:
            h = _swiglu(h, ffn_norm[layer], bf(ffn_w_gate[j]), bf(ffn_w_up[j]),
                        bf(ffn_w_down[j]), tm, 1408)
        else:
            h = _moe_layer(h, ffn_norm[layer], moe_router[j], bf(moe_w_gate[j]),
                           bf(moe_w_up[j]), bf(moe_w_down[j]), tm)
    return h.reshape(b, s, d)
```

```python
import functools
import math

import jax
import jax.numpy as jnp
from jax import lax
from jax.experimental import pallas as pl
from jax.experimental.pallas import tpu as pltpu

F32 = jnp.float32
BF16 = jnp.bfloat16
HIGHEST = lax.Precision.HIGHEST

EPS = 1e-6
ROPE_THETA = 10000.0
LANES = 128
SUBLANES = 8
VMEM_LIMIT_BYTES = 56 * 1024 * 1024

ATT_HEADS = 8
ATT_HEAD_DIM = 64
ATT_WIDTH = ATT_HEADS * ATT_HEAD_DIM
POOL_WINDOWS = (2, 4, 8, 16)
POOL_GROUP_DIM = 128
POOL_HALO = 16
MOBA_BLOCK = 256
MOBA_TOPK = 3
MOBA_QUERY_TILE = 128
PREP_BLOCKS_PER_STEP = 4
LOG2_E = 1.4426950408889634
NEG_BIG = -1e30

SSD_D_INNER = 2048
SSD_HEAD_DIM = 64
SSD_HEADS = 32
SSD_GROUPS = 4
SSD_STATE = 128
SSD_CONV = 4
SSD_CHUNK = 128
SSD_CONV_DIM = SSD_D_INNER + 2 * SSD_GROUPS * SSD_STATE
SSD_DT_PAD = 256
SSD_PROJ_TILE = 1792

XA_HEADS = 4
XA_HEAD_DIM = 256

N_EXPERTS = 8
TOP_K = 2
MOE_ROWS = 512
MOE_FF_CHUNK = 896
COMBINE_TOKENS = 256


def _cparams(*sem):
    return pltpu.CompilerParams(dimension_semantics=sem, vmem_limit_bytes=VMEM_LIMIT_BYTES)


def _rms(x, g):
    return x * lax.rsqrt(jnp.mean(x * x, axis=-1, keepdims=True) + EPS) * g


def _silu(x):
    return x / (1.0 + jnp.exp(-x))


def _iota(shape, dim):
    return lax.broadcasted_iota(jnp.int32, shape, dim)


def _norm_matmul_body(x_ref, g_ref, w_ref, o_ref, xn_ref):
    @pl.when(pl.program_id(1) == 0)
    def _():
        xn_ref[...] = _rms(x_ref[...], g_ref[...]).astype(BF16)

    o_ref[...] = jnp.dot(xn_ref[...], w_ref[...], preferred_element_type=F32)


def _norm_matmul(x, gain, w, tm, tn):
    n, d = x.shape
    f = w.shape[1]
    return pl.pallas_call(
        _norm_matmul_body,
        out_shape=jax.ShapeDtypeStruct((n, f), F32),
        grid=(n // tm, f // tn),
        in_specs=[pl.BlockSpec((tm, d), lambda i, j: (i, 0)),
                  pl.BlockSpec((1, d), lambda i, j: (0, 0)),
                  pl.BlockSpec((d, tn), lambda i, j: (0, j))],
        out_specs=pl.BlockSpec((tm, tn), lambda i, j: (i, j)),
        scratch_shapes=[pltpu.VMEM((tm, d), BF16)],
        compiler_params=_cparams("parallel", "arbitrary"),
        name="norm_matmul",
    )(x, gain.reshape(1, d), w)


def _matmul_residual_body(n_pairs, res_ref, *refs):
    o_ref = refs[-1]
    acc = res_ref[...]
    for p in range(n_pairs):
        acc = acc + jnp.dot(refs[2 * p][...].astype(BF16), refs[2 * p + 1][...],
                            preferred_element_type=F32)
    o_ref[...] = acc


def _matmul_residual(res, pairs, tm):
    n, d = res.shape
    in_specs = [pl.BlockSpec((tm, d), lambda i: (i, 0))]
    args = [res]
    for a, w in pairs:
        in_specs.append(pl.BlockSpec((tm, a.shape[1]), lambda i: (i, 0)))
        in_specs.append(pl.BlockSpec(w.shape, lambda i: (0, 0)))
        args += [a, w]
    return pl.pallas_call(
        functools.partial(_matmul_residual_body, len(pairs)),
        out_shape=jax.ShapeDtypeStruct((n, d), F32),
        grid=(n // tm,),
        in_specs=in_specs,
        out_specs=pl.BlockSpec((tm, d), lambda i: (i, 0)),
        compiler_params=_cparams("parallel"),
        name="matmul_residual",
    )(*args)


def _qkv_prep_body(q_ref, k_ref, v_ref, cos_ref, sin_ref, qg_ref, kg_ref,
                   qo_ref, ko_ref, vt_ref, km_ref):
    lane = _iota((1, LANES), 1)
    first_half = (lane % ATT_HEAD_DIM) < (ATT_HEAD_DIM // 2)
    r = _iota((LANES, LANES), 0) // ATT_HEAD_DIM
    c = _iota((LANES, LANES), 1) // ATT_HEAD_DIM
    head_mean = jnp.where(r == c, 1.0 / ATT_HEAD_DIM, 0.0).astype(BF16)
    cos = cos_ref[...]
    sin = sin_ref[...]

    def prep(x, g):
        sq = x * x
        hi = sq.astype(BF16)
        lo = (sq - hi.astype(F32)).astype(BF16)
        ms = (jnp.dot(hi, head_mean, preferred_element_type=F32)
              + jnp.dot(lo, head_mean, preferred_element_type=F32))
        xn = x * lax.rsqrt(ms + EPS) * g
        half = ATT_HEAD_DIM // 2
        partner = jnp.where(first_half, pltpu.roll(xn, LANES - half, 1), pltpu.roll(xn, half, 1))
        return xn * cos + partner * sin

    qo_ref[0] = prep(q_ref[0], qg_ref[...])
    kk = prep(k_ref[0], kg_ref[...])
    ko_ref[0] = kk.astype(BF16)
    v = v_ref[0]
    for r0 in range(v.shape[0] // MOBA_BLOCK):
        rows = slice(r0 * MOBA_BLOCK, (r0 + 1) * MOBA_BLOCK)
        km_ref[0, r0] = jnp.mean(kk[rows], axis=0, keepdims=True)
        vt_ref[0, r0] = v[rows].T.astype(BF16)


def _qkv_prep(proj, cos, sin, qg, kg):
    b, s, _ = proj.shape
    nb = s // MOBA_BLOCK
    n_pairs = ATT_WIDTH // LANES
    per_step = min(PREP_BLOCKS_PER_STEP, nb)
    rows = per_step * MOBA_BLOCK
    blk = (1, rows, LANES)
    qn, kn, vt, km = pl.pallas_call(
        _qkv_prep_body,
        out_shape=[jax.ShapeDtypeStruct((b, s, ATT_WIDTH), F32),
                   jax.ShapeDtypeStruct((b, s, ATT_WIDTH), BF16),
                   jax.ShapeDtypeStruct((b, nb, ATT_WIDTH, MOBA_BLOCK), BF16),
                   jax.ShapeDtypeStruct((b, nb, 1, ATT_WIDTH), F32)],
        grid=(b, nb // per_step, n_pairs),
        in_specs=[pl.BlockSpec(blk, lambda bi, i, p: (bi, i, p)),
                  pl.BlockSpec(blk, lambda bi, i, p: (bi, i, n_pairs + p)),
                  pl.BlockSpec(blk, lambda bi, i, p: (bi, i, 2 * n_pairs + p)),
                  pl.BlockSpec((rows, LANES), lambda bi, i, p: (i, 0)),
                  pl.BlockSpec((rows, LANES), lambda bi, i, p: (i, 0)),
                  pl.BlockSpec((1, LANES), lambda bi, i, p: (0, 0)),
                  pl.BlockSpec((1, LANES), lambda bi, i, p: (0, 0))],
        out_specs=[pl.BlockSpec(blk, lambda bi, i, p: (bi, i, p)),
                   pl.BlockSpec(blk, lambda bi, i, p: (bi, i, p)),
                   pl.BlockSpec((1, per_step, LANES, MOBA_BLOCK), lambda bi, i, p: (bi, i, p, 0)),
                   pl.BlockSpec((1, per_step, 1, LANES), lambda bi, i, p: (bi, i, 0, p))],
        compiler_params=_cparams("parallel", "parallel", "parallel"),
        name="moba_qkv_prep",
    )(proj, proj, proj, cos, sin, qg, kg)
    return qn, kn, vt, km.reshape(b, nb, ATT_WIDTH)


def _moba_body(q_ref, k_ref, vt_ref, km_ref, o_ref, bias_ref, s_ref, p_ref):
    i = pl.program_id(2)
    bs = MOBA_BLOCK
    qt = MOBA_QUERY_TILE
    q = q_ref[0]
    km = km_ref[0]
    nb = km.shape[0]
    scale = ATT_HEAD_DIM ** -0.5 * LOG2_E
    lane = _iota((1, LANES), 1)
    blk = _iota((nb, 1), 0)
    nt = (((1,), (1,)), ((), ()))

    streams = []
    for hh in range(2):
        qh = jnp.where((lane // ATT_HEAD_DIM) == hh, q, 0.0)
        gate = lax.dot_general(km, qh, nt, precision=HIGHEST, preferred_element_type=F32)
        gate = jnp.where(blk < i, gate, -jnp.inf)
        cnt = jnp.zeros((nb, bs), F32)
        for jp in range(nb):
            gj = gate[jp:jp + 1, :]
            cnt = cnt + jnp.where(jp < blk, jnp.where(gj >= gate, 1.0, 0.0),
                                  jnp.where(gj > gate, 1.0, 0.0))
        keep = jnp.logical_and(blk < i, cnt < MOBA_TOPK)
        sel_bias = jnp.where(keep, 0.0, NEG_BIG)
        for jp in range(nb):
            bias_ref[hh, jp] = sel_bias[jp:jp + 1, :]
        qb_all = (qh * scale).astype(BF16)
        vrows = slice(hh * ATT_HEAD_DIM, (hh + 1) * ATT_HEAD_DIM)
        for qs in range(0, bs, qt):
            streams.append((hh, qs, qb_all[qs:qs + qt, :], vrows))
    n_streams = len(streams)

    def score_tiles(j):
        k_j = k_ref[0, pl.ds(pl.multiple_of(j * bs, bs), bs), :]
        return tuple(lax.dot_general(k_j, qb, nt, preferred_element_type=F32)
                     for (_, _, qb, _) in streams)

    def value_products(j, probs):
        return [jnp.dot(vt_ref[0, j, vrows, :], probs[si], preferred_element_type=F32)
                for si, (_, _, _, vrows) in enumerate(streams)]

    ms, ls, accs = [], [], []
    for si, s in enumerate(score_tiles(i)):
        qs = streams[si][1]
        causal = _iota((bs, qt), 0) <= qs + _iota((bs, qt), 1)
        s = jnp.where(causal, s, NEG_BIG)
        m = jnp.max(s, axis=0, keepdims=True)
        p = jnp.exp2(s - m)
        ms.append(m)
        ls.append(jnp.sum(p, axis=0, keepdims=True))
        accs.append(jnp.zeros((ATT_HEAD_DIM, qt), F32))
        p_ref[1, si] = p.astype(BF16)
    for si, s in enumerate(score_tiles(0)):
        s_ref[0, si] = s

    def handle(j, slot, state):
        ms, ls, accs = state
        for si, s in enumerate(score_tiles(jnp.minimum(j + 1, nb - 1))):
            s_ref[1 - slot, si] = s
        owed = value_products(jnp.where(j == 0, i, j - 1),
                              [p_ref[1 - slot, si] for si in range(n_streams)])
        ms_o, ls_o, accs_o = [], [], []
        for si, (hh, qs, _, _) in enumerate(streams):
            s = s_ref[slot, si] + bias_ref[hh, j, :, qs:qs + qt]
            m_new = jnp.maximum(ms[si], jnp.max(s, axis=0, keepdims=True))
            alpha = jnp.exp2(ms[si] - m_new)
            p = jnp.exp2(s - m_new)
            ms_o.append(m_new)
            ls_o.append(alpha * ls[si] + jnp.sum(p, axis=0, keepdims=True))
            accs_o.append(alpha * (accs[si] + owed[si]))
            p_ref[slot, si] = p.astype(BF16)
        return tuple(ms_o), tuple(ls_o), tuple(accs_o)

    def two_blocks(t, state):
        return handle(2 * t + 1, 1, handle(2 * t, 0, state))

    trips = (i + 1) // 2
    ms, ls, accs = lax.fori_loop(0, trips, two_blocks, (tuple(ms), tuple(ls), tuple(accs)))
    owed = value_products(jnp.where(trips == 0, i, 2 * trips - 1),
                          [p_ref[1, si] for si in range(n_streams)])
    outs = [(accs[si] + owed[si]) / ls[si] for si in range(n_streams)]
    per_head = bs // qt
    out_rows = [jnp.concatenate(outs[hh * per_head:(hh + 1) * per_head], axis=1) for hh in range(2)]
    o_ref[0] = jnp.concatenate(out_rows, axis=0).T


def _moba_attention(qn, kn, vt, kmean):
    b, s, _ = qn.shape
    nb = s // MOBA_BLOCK
    n_pairs = ATT_WIDTH // LANES
    n_streams = 2 * MOBA_BLOCK // MOBA_QUERY_TILE
    return pl.pallas_call(
        _moba_body,
        out_shape=jax.ShapeDtypeStruct((b, s, ATT_WIDTH), F32),
        grid=(b, n_pairs, nb),
        in_specs=[pl.BlockSpec((1, MOBA_BLOCK, LANES), lambda bi, p, i: (bi, i, p)),
                  pl.BlockSpec((1, s, LANES), lambda bi, p, i: (bi, 0, p)),
                  pl.BlockSpec((1, nb, LANES, MOBA_BLOCK), lambda bi, p, i: (bi, 0, p, 0)),
                  pl.BlockSpec((1, nb, LANES), lambda bi, p, i: (bi, 0, p))],
        out_specs=pl.BlockSpec((1, MOBA_BLOCK, LANES), lambda bi, p, i: (bi, i, p)),
        scratch_shapes=[pltpu.VMEM((2, nb, 1, MOBA_BLOCK), F32),
                        pltpu.VMEM((2, n_streams, MOBA_BLOCK, MOBA_QUERY_TILE), F32),
                        pltpu.VMEM((2, n_streams, MOBA_BLOCK, MOBA_QUERY_TILE), BF16)],
        compiler_params=_cparams("parallel", "parallel", "arbitrary"),
        name="moba_attention",
    )(qn, kn, vt, kmean)


def _pool_body(ts, u_ref, halo_ref, w_ref, sc_ref, o_ref, ext_ref):
    i = pl.program_id(1)
    u = u_ref[0]
    ext_ref[0:POOL_HALO, :] = jnp.where(i > 0, halo_ref[0], 0.0)
    ext_ref[POOL_HALO:POOL_HALO + ts, :] = u
    t = i * ts + _iota((ts, 1), 0)
    outs = []
    for g, win in enumerate(POOL_WINDOWS):
        cols = slice(g * POOL_GROUP_DIM, (g + 1) * POOL_GROUP_DIM)
        ug = u[:, cols]
        acc = ug
        for k in range(1, win):
            acc = acc + ext_ref[POOL_HALO - k:POOL_HALO - k + ts, cols]
        cnt = jnp.minimum(t + 1, win).astype(F32)
        pooled = acc / cnt - ug
        outs.append(jnp.dot(pooled.astype(BF16), w_ref[g], preferred_element_type=F32))
    o_ref[0] = jnp.concatenate(outs, axis=1) * sc_ref[...]


def _multiscale_pool(proj, w_pool, pool_scale, ts):
    b, s, f = proj.shape
    width = len(POOL_WINDOWS) * POOL_GROUP_DIM
    assert max(POOL_WINDOWS) <= POOL_HALO and f % width == 0
    col = f // width - 1
    halo_per_tile = ts // POOL_HALO
    return pl.pallas_call(
        functools.partial(_pool_body, ts),
        out_shape=jax.ShapeDtypeStruct((b, s, width), F32),
        grid=(b, s // ts),
        in_specs=[pl.BlockSpec((1, ts, width), lambda bi, i: (bi, i, col)),
                  pl.BlockSpec((1, POOL_HALO, width),
                               lambda bi, i: (bi, jnp.maximum(i * halo_per_tile - 1, 0), col)),
                  pl.BlockSpec(w_pool.shape, lambda bi, i: (0, 0, 0)),
                  pl.BlockSpec((1, width), lambda bi, i: (0, 0))],
        out_specs=pl.BlockSpec((1, ts, width), lambda bi, i: (bi, i, 0)),
        scratch_shapes=[pltpu.VMEM((POOL_HALO + ts, width), F32)],
        compiler_params=_cparams("parallel", "parallel"),
        name="multiscale_pool",
    )(proj, proj, w_pool, pool_scale.reshape(1, width))


def _xattn_body(h_ref, kv_ref, g_ref, wq_ref, qg_ref, kg_ref, wo_ref, o_ref):
    h = h_ref[0]
    d = h.shape[-1]
    kv = kv_ref[0]
    hn = _rms(h, g_ref[...]).astype(BF16)
    q = jnp.dot(hn, wq_ref[...], preferred_element_type=F32)
    scale = XA_HEAD_DIM ** -0.5
    nt = (((1,), (1,)), ((), ()))
    outs = []
    for hh in range(XA_HEADS):
        cols = slice(hh * XA_HEAD_DIM, (hh + 1) * XA_HEAD_DIM)
        qh = _rms(q[:, cols], qg_ref[...]).astype(BF16)
        kh = _rms(kv[:, cols], kg_ref[...]).astype(BF16)
        vh = kv[:, d + hh * XA_HEAD_DIM:d + (hh + 1) * XA_HEAD_DIM].astype(BF16)
        s = lax.dot_general(qh, kh, nt, preferred_element_type=F32) * scale
        m = jnp.max(s, axis=-1, keepdims=True)
        p = jnp.exp(s - m)
        p = p / jnp.sum(p, axis=-1, keepdims=True)
        outs.append(jnp.dot(p.astype(BF16), vh, preferred_element_type=F32))
    o = jnp.concatenate(outs, axis=1).astype(BF16)
    o_ref[0] = h + jnp.dot(o, wo_ref[...], preferred_element_type=F32)


def _cross_attention(h, kv, gain, wq, q_gain, k_gain, wo, ts):
    b, s, d = h.shape
    m = kv.shape[1]
    return pl.pallas_call(
        _xattn_body,
        out_shape=jax.ShapeDtypeStruct((b, s, d), F32),
        grid=(b, s // ts),
        in_specs=[pl.BlockSpec((1, ts, d), lambda bi, i: (bi, i, 0)),
                  pl.BlockSpec((1, m, 2 * d), lambda bi, i: (bi, 0, 0)),
                  pl.BlockSpec((1, d), lambda bi, i: (0, 0)),
                  pl.BlockSpec((d, d), lambda bi, i: (0, 0)),
                  pl.BlockSpec((1, XA_HEAD_DIM), lambda bi, i: (0, 0)),
                  pl.BlockSpec((1, XA_HEAD_DIM), lambda bi, i: (0, 0)),
                  pl.BlockSpec((d, d), lambda bi, i: (0, 0))],
        out_specs=pl.BlockSpec((1, ts, d), lambda bi, i: (bi, i, 0)),
        compiler_params=_cparams("parallel", "parallel"),
        name="memory_cross_attention",
    )(h, kv, gain.reshape(1, d), wq, q_gain.reshape(1, -1), k_gain.reshape(1, -1), wo)


def _swiglu_body(h_ref, g_ref, wg_ref, wu_ref, wd_ref, o_ref, hn_ref, acc_ref):
    j = pl.program_id(1)

    @pl.when(j == 0)
    def _():
        hn_ref[...] = _rms(h_ref[...], g_ref[...]).astype(BF16)
        acc_ref[...] = h_ref[...]

    hn = hn_ref[...]
    gate = jnp.dot(hn, wg_ref[...], preferred_element_type=F32)
    up = jnp.dot(hn, wu_ref[...], preferred_element_type=F32)
    act = (_silu(gate) * up).astype(BF16)
    acc_ref[...] += jnp.dot(act, wd_ref[...], preferred_element_type=F32)

    @pl.when(j == pl.num_programs(1) - 1)
    def _():
        o_ref[...] = acc_ref[...]


def _swiglu(h, gain, wg, wu, wd, tm, tf):
    n, d = h.shape
    f = wg.shape[1]
    return pl.pallas_call(
        _swiglu_body,
        out_shape=jax.ShapeDtypeStruct((n, d), F32),
        grid=(n // tm, f // tf),
        in_specs=[pl.BlockSpec((tm, d), lambda i, j: (i, 0)),
                  pl.BlockSpec((1, d), lambda i, j: (0, 0)),
                  pl.BlockSpec((d, tf), lambda i, j: (0, j)),
                  pl.BlockSpec((d, tf), lambda i, j: (0, j)),
                  pl.BlockSpec((tf, d), lambda i, j: (j, 0))],
        out_specs=pl.BlockSpec((tm, d), lambda i, j: (i, 0)),
        scratch_shapes=[pltpu.VMEM((tm, d), BF16), pltpu.VMEM((tm, d), F32)],
        compiler_params=_cparams("parallel", "arbitrary"),
        name="swiglu",
    )(h, gain.reshape(1, d), wg, wu, wd)


def _ssd_body(zx_ref, cw_ref, cb_ref, dtb_ref, alog_ref, dskip_ref, ng_ref, o_ref,
              prev_ref, state_ref):
    c = pl.program_id(1)
    L = SSD_CHUNK
    n_state = SSD_STATE
    pair_w = 2 * SSD_HEAD_DIM
    assert pair_w == LANES and n_state == LANES and L == LANES

    @pl.when(c == 0)
    def _():
        prev_ref[...] = jnp.zeros_like(prev_ref)
        state_ref[...] = jnp.zeros_like(state_ref)

    blk = zx_ref[0]
    z = blk[:, :SSD_D_INNER]
    xr = blk[:, SSD_D_INNER:SSD_D_INNER + SSD_CONV_DIM]
    dtr = blk[:, SSD_D_INNER + SSD_CONV_DIM:SSD_D_INNER + SSD_CONV_DIM + LANES]

    tail = prev_ref[...]
    head = xr[0:SUBLANES]
    row = _iota((SUBLANES, 1), 0)
    conv = xr * cw_ref[SSD_CONV - 1:SSD_CONV, :]
    conv_head = head * cw_ref[SSD_CONV - 1:SSD_CONV, :]
    for k in range(1, SSD_CONV):
        tap = cw_ref[SSD_CONV - 1 - k:SSD_CONV - k, :]
        conv = conv + pltpu.roll(xr, k, 0) * tap
        conv_head = conv_head + jnp.where(row < k, pltpu.roll(tail, k, 0),
                                          pltpu.roll(head, k, 0)) * tap
    conv = jnp.concatenate([conv_head, conv[SUBLANES:]], axis=0)
    prev_ref[...] = xr[L - SUBLANES:L]
    xa = _silu(conv + cb_ref[...])
    xs = xa[:, :SSD_D_INNER]
    bm = xa[:, SSD_D_INNER:SSD_D_INNER + SSD_GROUPS * n_state]
    cm = xa[:, SSD_D_INNER + SSD_GROUPS * n_state:]

    dtx = dtr + dtb_ref[...]
    dt = jnp.maximum(dtx, 0.0) + jnp.log(1.0 + jnp.exp(-jnp.abs(dtx)))
    da = dt * (-jnp.exp(alog_ref[...]))
    tri = jnp.where(_iota((L, L), 1) <= _iota((L, L), 0), 1.0, 0.0)
    acum = jnp.dot(tri, da, precision=HIGHEST, preferred_element_type=F32)
    acum_t = acum.T
    dt_t = dt.T
    shifted_t = acum_t - jnp.log(dt_t)
    to_end_t = jnp.exp(acum_t[:, L - 1:L] - acum_t)
    causal = _iota((L, L), 1) <= _iota((L, L), 0)
    lane = _iota((1, LANES), 1)
    low = lane < SSD_HEAD_DIM
    nt = (((1,), (1,)), ((), ()))

    heads_per_group = SSD_HEADS // SSD_GROUPS
    y_pairs = []
    for g in range(SSD_GROUPS):
        bg = bm[:, g * n_state:(g + 1) * n_state]
        cg = cm[:, g * n_state:(g + 1) * n_state].astype(BF16)
        bg_t = bg.T
        cb = lax.dot_general(cg, bg.astype(BF16), nt, preferred_element_type=F32)
        for pp in range(heads_per_group // 2):
            pidx = g * (heads_per_group // 2) + pp
            cols = slice(pidx * pair_w, (pidx + 1) * pair_w)
            x_pair = xs[:, cols].astype(BF16)
            st_pair = state_ref[:, cols]
            cs = jnp.dot(cg, st_pair.astype(BF16), preferred_element_type=F32)
            ys, upds, lasts = [], [], []
            for hh in range(2):
                h = 2 * pidx + hh
                bc = jnp.broadcast_to(acum[:, h:h + 1], (L, L))
                seg = bc - shifted_t[h:h + 1, :]
                dec = jnp.exp(jnp.where(causal, seg, -jnp.inf))
                mm = (cb * dec).astype(BF16)
                ebc = jnp.exp(bc)
                ys.append(jnp.dot(mm, x_pair, preferred_element_type=F32) + ebc * cs)
                wrow = dt_t[h:h + 1, :] * to_end_t[h:h + 1, :]
                upds.append(jnp.dot((bg_t * wrow).astype(BF16), x_pair,
                                    preferred_element_type=F32))
                lasts.append(ebc[L - 1:L, :])
            y_pairs.append(jnp.where(low, ys[0], ys[1]))
            state_ref[:, cols] = (st_pair * jnp.where(low, lasts[0], lasts[1])
                                  + jnp.where(low, upds[0], upds[1]))

    y = jnp.concatenate(y_pairs, axis=1)
    y = (y + dskip_ref[...] * xs) * _silu(z)
    gw = SSD_D_INNER // SSD_GROUPS
    outs = []
    for g in range(SSD_GROUPS):
        cols = slice(g * gw, (g + 1) * gw)
        outs.append(_rms(y[:, cols], ng_ref[:, cols]))
    o_ref[0] = jnp.concatenate(outs, axis=1)


def _ssd_mixer(zx, conv_w, conv_b, dt_bias, a_log, d_skip, norm_g):
    b, s, f = zx.shape
    pad = LANES - SSD_HEADS
    dtb = jnp.pad(dt_bias, (0, pad)).reshape(1, LANES)
    alog = jnp.pad(a_log, (0, pad)).reshape(1, LANES)
    dskip = jnp.repeat(d_skip, SSD_HEAD_DIM).reshape(1, SSD_D_INNER)
    full = lambda shape: pl.BlockSpec(shape, lambda bi, c: (0,) * len(shape))
    return pl.pallas_call(
        _ssd_body,
        out_shape=jax.ShapeDtypeStruct((b, s, SSD_D_INNER), F32),
        grid=(b, s // SSD_CHUNK),
        in_specs=[pl.BlockSpec((1, SSD_CHUNK, f), lambda bi, c: (bi, c, 0)),
                  full((SSD_CONV, SSD_CONV_DIM)), full((1, SSD_CONV_DIM)),
                  full((1, LANES)), full((1, LANES)),
                  full((1, SSD_D_INNER)), full((1, SSD_D_INNER))],
        out_specs=pl.BlockSpec((1, SSD_CHUNK, SSD_D_INNER), lambda bi, c: (bi, c, 0)),
        scratch_shapes=[pltpu.VMEM((SUBLANES, SSD_CONV_DIM), F32),
                        pltpu.VMEM((SSD_STATE, SSD_D_INNER), F32)],
        compiler_params=_cparams("parallel", "arbitrary"),
        name="ssd_mixer",
    )(zx, conv_w, conv_b.reshape(1, -1), dtb, alog, dskip, norm_g.reshape(1, -1))


def _router_body(h_ref, g_ref, wr_ref, hn_ref, lg_ref):
    hn = _rms(h_ref[...], g_ref[...])
    _matrix_to_rows(hn_ref, hn)
    lg_ref[...] = jnp.dot(hn, wr_ref[...], precision=HIGHEST, preferred_element_type=F32)


def _router(h, gain, w_router_pad, tm):
    n, d = h.shape
    assert d == SUBLANES * LANES
    hn_rows, logits = pl.pallas_call(
        _router_body,
        out_shape=[jax.ShapeDtypeStruct((n * SUBLANES, LANES), F32),
                   jax.ShapeDtypeStruct((n, LANES), F32)],
        grid=(n // tm,),
        in_specs=[pl.BlockSpec((tm, d), lambda i: (i, 0)),
                  pl.BlockSpec((1, d), lambda i: (0, 0)),
                  pl.BlockSpec((d, LANES), lambda i: (0, 0))],
        out_specs=[pl.BlockSpec((tm * SUBLANES, LANES), lambda i: (i, 0)),
                   pl.BlockSpec((tm, LANES), lambda i: (i, 0))],
        compiler_params=_cparams("parallel"),
        name="moe_router",
    )(h, gain.reshape(1, d), w_router_pad)
    return hn_rows.reshape(n, SUBLANES, LANES), logits


def _matrix_to_rows(rows_ref, x):
    n = x.shape[0]
    for j in range(x.shape[1] // LANES):
        rows_ref[pl.ds(j, n, stride=SUBLANES), :] = x[:, j * LANES:(j + 1) * LANES]


def _row_copy(src_hbm, dst_ref, sems, slot, idx, r):
    rows = pl.ds(pl.multiple_of(r * SUBLANES, SUBLANES), SUBLANES)
    return pltpu.make_async_copy(src_hbm.at[idx], dst_ref.at[slot, rows, :], sems.at[slot])


def _start_row_gather(idx_ref, src_hbm, dst_ref, sems, slot, n_rows):
    def body(r, carry):
        _row_copy(src_hbm, dst_ref, sems, slot, idx_ref[0, 0, r], r).start()
        return carry

    lax.fori_loop(0, n_rows, body, 0, unroll=8)


def _wait_row_gather(src_hbm, dst_ref, sems, slot, n_rows):
    def body(r, carry):
        _row_copy(src_hbm, dst_ref, sems, slot, 0, r).wait()
        return carry

    lax.fori_loop(0, n_rows, body, 0, unroll=8)


def _rows_to_matrix(buf_ref, slot, first_row, n_rows, d):
    chunks = [buf_ref[slot, pl.ds(first_row * SUBLANES + j, n_rows, stride=SUBLANES), :]
              for j in range(d // LANES)]
    return jnp.concatenate(chunks, axis=1)


def _expert_body(blk_e_ref, n_valid_ref, tok_ref, tok_next_ref, hn_hbm, gate_ref,
                 wg_ref, wu_ref, wd_ref, o_ref, x_ref, sems):
    i = pl.program_id(0)
    last = pl.num_programs(0) - 1
    rows = gate_ref.shape[0]
    d = wg_ref.shape[1]
    slot = i % 2

    @pl.when(jnp.logical_and(i == 0, n_valid_ref[0] > 0))
    def _():
        _start_row_gather(tok_ref, hn_hbm, x_ref, sems, 0, rows)

    @pl.when(jnp.logical_and(i < last, n_valid_ref[jnp.minimum(i + 1, last)] > 0))
    def _():
        _start_row_gather(tok_next_ref, hn_hbm, x_ref, sems, 1 - slot, rows)

    @pl.when(n_valid_ref[i] > 0)
    def _():
        _wait_row_gather(hn_hbm, x_ref, sems, slot, rows)
        x = _rows_to_matrix(x_ref, slot, 0, rows, d).astype(BF16)
        f = wg_ref.shape[2]
        acc = jnp.zeros((rows, d), F32)
        for c0 in range(0, f, MOE_FF_CHUNK):
            cols = slice(c0, c0 + MOE_FF_CHUNK)
            gate = jnp.dot(x, wg_ref[0, :, cols], preferred_element_type=F32)
            up = jnp.dot(x, wu_ref[0, :, cols], preferred_element_type=F32)
            act = (_silu(gate) * up).astype(BF16)
            acc = acc + jnp.dot(act, wd_ref[0, cols, :], preferred_element_type=F32)
        _matrix_to_rows(o_ref, acc * gate_ref[...])

    @pl.when(n_valid_ref[i] == 0)
    def _():
        o_ref[...] = jnp.zeros_like(o_ref)


def _expert_ffn(blk_e, n_valid, row_tok, hn_rows, row_gate, wg, wu, wd):
    n_blocks = blk_e.shape[0]
    d = wg.shape[1]
    f = wg.shape[2]
    rows = MOE_ROWS
    tok = row_tok.reshape(n_blocks, 1, rows)
    grid_spec = pltpu.PrefetchScalarGridSpec(
        num_scalar_prefetch=2,
        grid=(n_blocks,),
        in_specs=[pl.BlockSpec((1, 1, rows), lambda i, be, nv: (i, 0, 0), memory_space=pltpu.SMEM),
                  pl.BlockSpec((1, 1, rows), lambda i, be, nv: (jnp.minimum(i + 1, n_blocks - 1), 0, 0),
                               memory_space=pltpu.SMEM),
                  pl.BlockSpec(memory_space=pl.ANY),
                  pl.BlockSpec((rows, 1), lambda i, be, nv: (i, 0)),
                  pl.BlockSpec((1, d, f), lambda i, be, nv: (be[i], 0, 0)),
                  pl.BlockSpec((1, d, f), lambda i, be, nv: (be[i], 0, 0)),
                  pl.BlockSpec((1, f, d), lambda i, be, nv: (be[i], 0, 0))],
        out_specs=pl.BlockSpec((rows * SUBLANES, LANES), lambda i, be, nv: (i, 0)),
        scratch_shapes=[pltpu.VMEM((2, rows * SUBLANES, LANES), F32),
                        pltpu.SemaphoreType.DMA((2,))],
    )
    y_rows = pl.pallas_call(
        _expert_body,
        out_shape=jax.ShapeDtypeStruct((n_blocks * rows * SUBLANES, LANES), F32),
        grid_spec=grid_spec,
        compiler_params=_cparams("arbitrary"),
        name="moe_expert_ffn",
    )(blk_e, n_valid, tok, tok, hn_rows, row_gate.reshape(-1, 1), wg, wu, wd)
    return y_rows.reshape(n_blocks * rows, SUBLANES, LANES)


def _combine_body(pos_ref, pos_next_ref, h_ref, y_hbm, o_ref, y_ref, sems):
    i = pl.program_id(0)
    last = pl.num_programs(0) - 1
    tc, d = o_ref.shape
    slot = i % 2

    @pl.when(i == 0)
    def _():
        _start_row_gather(pos_ref, y_hbm, y_ref, sems, 0, TOP_K * tc)

    @pl.when(i < last)
    def _():
        _start_row_gather(pos_next_ref, y_hbm, y_ref, sems, 1 - slot, TOP_K * tc)

    _wait_row_gather(y_hbm, y_ref, sems, slot, TOP_K * tc)
    acc = h_ref[...]
    for k in range(TOP_K):
        acc = acc + _rows_to_matrix(y_ref, slot, k * tc, tc, d)
    o_ref[...] = acc


def _moe_combine(h, pos, y_rows):
    n, d = h.shape
    tc = COMBINE_TOKENS
    n_tiles = n // tc
    pos_tiles = pos.reshape(n_tiles, tc, TOP_K).transpose(0, 2, 1).reshape(n_tiles, 1, TOP_K * tc)
    return pl.pallas_call(
        _combine_body,
        out_shape=jax.ShapeDtypeStruct((n, d), F32),
        grid=(n_tiles,),
        in_specs=[pl.BlockSpec((1, 1, TOP_K * tc), lambda i: (i, 0, 0), memory_space=pltpu.SMEM),
                  pl.BlockSpec((1, 1, TOP_K * tc), lambda i: (jnp.minimum(i + 1, n_tiles - 1), 0, 0),
                               memory_space=pltpu.SMEM),
                  pl.BlockSpec((tc, d), lambda i: (i, 0)),
                  pl.BlockSpec(memory_space=pl.ANY)],
        out_specs=pl.BlockSpec((tc, d), lambda i: (i, 0)),
        scratch_shapes=[pltpu.VMEM((2, TOP_K * tc * SUBLANES, LANES), F32),
                        pltpu.SemaphoreType.DMA((2,))],
        compiler_params=_cparams("arbitrary"),
        name="moe_combine",
    )(pos_tiles, pos_tiles, h, y_rows)


def _moe_layer(h, gain, w_router, wg, wu, wd, tm):
    n, d = h.shape
    e = w_router.shape[1]
    rows = MOE_ROWS
    n_slots = n * TOP_K
    hn_rows, logits = _router(h, gain, jnp.pad(w_router, ((0, 0), (0, LANES - e))), tm)
    top_logit, top_e = lax.top_k(logits[:, :e], TOP_K)
    gates = jax.nn.softmax(top_logit, axis=-1)

    e_flat = top_e.reshape(-1).astype(jnp.int32)
    slot_ids = jnp.arange(n_slots, dtype=jnp.int32)
    e_sorted, order = lax.sort((e_flat, slot_ids), num_keys=1, is_stable=True)
    experts = jnp.arange(e, dtype=jnp.int32)
    sorted_onehot = e_sorted[:, None] == experts[None, :]
    counts = jnp.sum(sorted_onehot, axis=0, dtype=jnp.int32)
    start = jnp.cumsum(counts) - counts
    padded = (counts + rows - 1) // rows * rows
    pad_end = jnp.cumsum(padded)
    pad_start = pad_end - padded
    shift = jnp.sum(jnp.where(sorted_onehot, (pad_start - start)[None, :], 0), axis=1)
    dest = slot_ids + shift
    _, pos = lax.sort((order, dest), num_keys=1)
    n_blocks = n_slots // rows + e
    n_rows = n_blocks * rows
    blk_start = jnp.arange(n_blocks, dtype=jnp.int32) * rows
    blk_e = jnp.minimum(jnp.sum(blk_start[:, None] >= pad_end[None, :], axis=1),
                        e - 1).astype(jnp.int32)
    n_valid = jnp.clip(pad_start[blk_e] + counts[blk_e] - blk_start, 0, rows).astype(jnp.int32)
    row_rank = (jnp.arange(rows, dtype=jnp.int32)[None, :] + (blk_start - pad_start[blk_e])[:, None])
    row_ok = row_rank < counts[blk_e][:, None]
    src = jnp.where(row_ok, start[blk_e][:, None] + row_rank, 0).reshape(-1)
    row_slot = order[src]
    row_tok = jnp.where(row_ok.reshape(-1), row_slot // TOP_K, 0)
    row_gate = jnp.where(row_ok.reshape(-1), gates.reshape(-1)[row_slot], 0.0)

    y_rows = _expert_ffn(blk_e, n_valid, row_tok, hn_rows, row_gate, wg, wu, wd)
    return _moe_combine(h, pos, y_rows)


def _rope_tables(positions):
    half = ATT_HEAD_DIM // 2
    inv_freq = ROPE_THETA ** (-jnp.arange(half, dtype=F32) / half)
    ang = positions.astype(F32)[:, None] * inv_freq[None, :]
    cos = jnp.cos(ang)
    sin = jnp.sin(ang)
    reps = LANES // ATT_HEAD_DIM
    return (jnp.tile(cos, (1, 2 * reps)), jnp.tile(jnp.concatenate([-sin, sin], axis=1), (1, reps)))


def kernel(x, mem, positions, mix_norm, xa_norm, mem_norm, ffn_norm, xa_wq, xa_wkv, xa_q_norm, xa_k_norm, xa_wo, hy_w_in, hy_q_norm, hy_k_norm, pool_w, pool_scale, hy_w_out, ffn_w_gate, ffn_w_up, ffn_w_down, ssd_w_in, ssd_conv_w, ssd_conv_b, ssd_dt_bias, ssd_a_log, ssd_d, ssd_norm, ssd_w_out, moe_router, moe_w_gate, moe_w_up, moe_w_down):
    b, s, d = x.shape
    n = b * s
    n_mem = mem.shape[1]
    depth = mix_norm.shape[0]
    tm = min(512, n)
    ts = min(512, s)
    bf = lambda w: w.astype(BF16)
    cos, sin = _rope_tables(positions)
    reps = LANES // ATT_HEAD_DIM

    h = x.reshape(n, d)
    mem2 = mem.reshape(b * n_mem, d)
    for layer in range(depth):
        j = layer // 2
        if layer % 2 == 0:
            proj = _norm_matmul(h, mix_norm[layer], bf(hy_w_in[j]), min(1024, n), 2048).reshape(b, s, -1)
            qn, kn, vt, kmean = _qkv_prep(proj, cos, sin,
                                          jnp.tile(hy_q_norm[j], reps).reshape(1, LANES),
                                          jnp.tile(hy_k_norm[j], reps).reshape(1, LANES))
            att = _moba_attention(qn, kn, vt, kmean)
            pooled = _multiscale_pool(proj, bf(pool_w[j]), pool_scale[j], ts)
            w_out = bf(hy_w_out[j])
            h = _matmul_residual(h, [(att.reshape(n, -1), w_out[:ATT_WIDTH]),
                                     (pooled.reshape(n, -1), w_out[ATT_WIDTH:])], tm)
        else:
            w_in = ssd_w_in[j]
            w_cat = jnp.pad(w_in, ((0, 0), (0, SSD_DT_PAD - SSD_HEADS)))
            zx = _norm_matmul(h, mix_norm[layer], bf(w_cat), min(1024, n), SSD_PROJ_TILE).reshape(b, s, -1)
            y = _ssd_mixer(zx, ssd_conv_w[j], ssd_conv_b[j], ssd_dt_bias[j], ssd_a_log[j],
                           ssd_d[j], ssd_norm[j])
            h = _matmul_residual(h, [(y.reshape(n, -1), bf(ssd_w_out[j]))], tm)

        kv = _norm_matmul(mem2, mem_norm[layer], bf(xa_wkv[layer]), min(512, b * n_mem), 1024)
        h = _cross_attention(h.reshape(b, s, d), kv.reshape(b, n_mem, 2 * d), xa_norm[layer],
                             bf(xa_wq[layer]), xa_q_norm[layer], xa_k_norm[layer],
                             bf(xa_wo[layer]), ts).reshape(n, d)

        if layer % 2 == 0:
            h = _swiglu(h, ffn_norm[layer], bf(ffn_w_gate[j]), bf(ffn_w_up[j]),
                        bf(ffn_w_down[j]), tm, 1408)
        else:
            h = _moe_layer(h, ffn_norm[layer], moe_router[j], bf(moe_w_gate[j]),
                           bf(moe_w_up[j]), bf(moe_w_down[j]), tm)
    return h.reshape(b, s, d)
```

```python
import functools
import math

import jax
import jax.numpy as jnp
from jax import lax
from jax.experimental import pallas as pl
from jax.experimental.pallas import tpu as pltpu

F32 = jnp.float32
BF16 = jnp.bfloat16
HIGHEST = lax.Precision.HIGHEST

EPS = 1e-6
ROPE_THETA = 10000.0
LANES = 128
SUBLANES = 8
VMEM_LIMIT_BYTES = 56 * 1024 * 1024

ATT_HEADS = 8
ATT_HEAD_DIM = 64
ATT_WIDTH = ATT_HEADS * ATT_HEAD_DIM
POOL_WINDOWS = (2, 4, 8, 16)
POOL_GROUP_DIM = 128
POOL_HALO = 16
MOBA_BLOCK = 256
MOBA_TOPK = 3
MOBA_QUERY_TILE = 128
PREP_BLOCKS_PER_STEP = 4
LOG2_E = 1.4426950408889634
NEG_BIG = -1e30

SSD_D_INNER = 2048
SSD_HEAD_DIM = 64
SSD_HEADS = 32
SSD_GROUPS = 4
SSD_STATE = 128
SSD_CONV = 4
SSD_CHUNK = 128
SSD_CONV_DIM = SSD_D_INNER + 2 * SSD_GROUPS * SSD_STATE
SSD_DT_PAD = 256
SSD_PROJ_TILE = 1792

XA_HEADS = 4
XA_HEAD_DIM = 256

N_EXPERTS = 8
TOP_K = 2
MOE_ROWS = 512
FF_CHUNK = 512
COMBINE_TOKENS = 256


def _cparams(*sem):
    return pltpu.CompilerParams(dimension_semantics=sem, vmem_limit_bytes=VMEM_LIMIT_BYTES)


def _rms(x, g):
    return x * lax.rsqrt(jnp.mean(x * x, axis=-1, keepdims=True) + EPS) * g


def _silu(x):
    return x / (1.0 + jnp.exp(-x))


def _iota(shape, dim):
    return lax.broadcasted_iota(jnp.int32, shape, dim)


def _norm_matmul_body(x_ref, g_ref, w_ref, o_ref, xn_ref):
    @pl.when(pl.program_id(1) == 0)
    def _():
        xn_ref[...] = _rms(x_ref[...], g_ref[...]).astype(BF16)

    o_ref[...] = jnp.dot(xn_ref[...], w_ref[...], preferred_element_type=F32)


def _norm_matmul(x, gain, w, tm, tn):
    n, d = x.shape
    f = w.shape[1]
    return pl.pallas_call(
        _norm_matmul_body,
        out_shape=jax.ShapeDtypeStruct((n, f), F32),
        grid=(n // tm, f // tn),
        in_specs=[pl.BlockSpec((tm, d), lambda i, j: (i, 0)),
                  pl.BlockSpec((1, d), lambda i, j: (0, 0)),
                  pl.BlockSpec((d, tn), lambda i, j: (0, j))],
        out_specs=pl.BlockSpec((tm, tn), lambda i, j: (i, j)),
        scratch_shapes=[pltpu.VMEM((tm, d), BF16)],
        compiler_params=_cparams("parallel", "arbitrary"),
        name="norm_matmul",
    )(x, gain.reshape(1, d), w)


def _matmul_residual_body(n_pairs, res_ref, *refs):
    o_ref = refs[-1]
    acc = res_ref[...]
    for p in range(n_pairs):
        acc = acc + jnp.dot(refs[2 * p][...].astype(BF16), refs[2 * p + 1][...],
                            preferred_element_type=F32)
    o_ref[...] = acc


def _matmul_residual(res, pairs, tm):
    n, d = res.shape
    in_specs = [pl.BlockSpec((tm, d), lambda i: (i, 0))]
    args = [res]
    for a, w in pairs:
        in_specs.append(pl.BlockSpec((tm, a.shape[1]), lambda i: (i, 0)))
        in_specs.append(pl.BlockSpec(w.shape, lambda i: (0, 0)))
        args += [a, w]
    return pl.pallas_call(
        functools.partial(_matmul_residual_body, len(pairs)),
        out_shape=jax.ShapeDtypeStruct((n, d), F32),
        grid=(n // tm,),
        in_specs=in_specs,
        out_specs=pl.BlockSpec((tm, d), lambda i: (i, 0)),
        compiler_params=_cparams("parallel"),
        name="matmul_residual",
    )(*args)


def _qkv_prep_body(q_ref, k_ref, v_ref, cos_ref, sin_ref, qg_ref, kg_ref,
                   qo_ref, ko_ref, vt_ref, km_ref):
    lane = _iota((1, LANES), 1)
    first_half = (lane % ATT_HEAD_DIM) < (ATT_HEAD_DIM // 2)
    r = _iota((LANES, LANES), 0) // ATT_HEAD_DIM
    c = _iota((LANES, LANES), 1) // ATT_HEAD_DIM
    head_mean = jnp.where(r == c, 1.0 / ATT_HEAD_DIM, 0.0).astype(BF16)
    cos = cos_ref[...]
    sin = sin_ref[...]

    def prep(x, g):
        sq = x * x
        hi = sq.astype(BF16)
        lo = (sq - hi.astype(F32)).astype(BF16)
        ms = (jnp.dot(hi, head_mean, preferred_element_type=F32)
              + jnp.dot(lo, head_mean, preferred_element_type=F32))
        xn = x * lax.rsqrt(ms + EPS) * g
        half = ATT_HEAD_DIM // 2
        partner = jnp.where(first_half, pltpu.roll(xn, LANES - half, 1), pltpu.roll(xn, half, 1))
        return xn * cos + partner * sin

    qo_ref[0] = prep(q_ref[0], qg_ref[...])
    kk = prep(k_ref[0], kg_ref[...])
    ko_ref[0] = kk.astype(BF16)
    v = v_ref[0]
    for r0 in range(v.shape[0] // MOBA_BLOCK):
        rows = slice(r0 * MOBA_BLOCK, (r0 + 1) * MOBA_BLOCK)
        km_ref[0, r0] = jnp.mean(kk[rows], axis=0, keepdims=True)
        vt_ref[0, r0] = v[rows].T.astype(BF16)


def _qkv_prep(proj, cos, sin, qg, kg):
    b, s, _ = proj.shape
    nb = s // MOBA_BLOCK
    n_pairs = ATT_WIDTH // LANES
    per_step = min(PREP_BLOCKS_PER_STEP, nb)
    rows = per_step * MOBA_BLOCK
    blk = (1, rows, LANES)
    qn, kn, vt, km = pl.pallas_call(
        _qkv_prep_body,
        out_shape=[jax.ShapeDtypeStruct((b, s, ATT_WIDTH), F32),
                   jax.ShapeDtypeStruct((b, s, ATT_WIDTH), BF16),
                   jax.ShapeDtypeStruct((b, nb, ATT_WIDTH, MOBA_BLOCK), BF16),
                   jax.ShapeDtypeStruct((b, nb, 1, ATT_WIDTH), F32)],
        grid=(b, nb // per_step, n_pairs),
        in_specs=[pl.BlockSpec(blk, lambda bi, i, p: (bi, i, p)),
                  pl.BlockSpec(blk, lambda bi, i, p: (bi, i, n_pairs + p)),
                  pl.BlockSpec(blk, lambda bi, i, p: (bi, i, 2 * n_pairs + p)),
                  pl.BlockSpec((rows, LANES), lambda bi, i, p: (i, 0)),
                  pl.BlockSpec((rows, LANES), lambda bi, i, p: (i, 0)),
                  pl.BlockSpec((1, LANES), lambda bi, i, p: (0, 0)),
                  pl.BlockSpec((1, LANES), lambda bi, i, p: (0, 0))],
        out_specs=[pl.BlockSpec(blk, lambda bi, i, p: (bi, i, p)),
                   pl.BlockSpec(blk, lambda bi, i, p: (bi, i, p)),
                   pl.BlockSpec((1, per_step, LANES, MOBA_BLOCK), lambda bi, i, p: (bi, i, p, 0)),
                   pl.BlockSpec((1, per_step, 1, LANES), lambda bi, i, p: (bi, i, 0, p))],
        compiler_params=_cparams("parallel", "parallel", "parallel"),
        name="moba_qkv_prep",
    )(proj, proj, proj, cos, sin, qg, kg)
    return qn, kn, vt, km.reshape(b, nb, ATT_WIDTH)


def _moba_select_body(q_ref, km_ref, o_ref):
    q = q_ref[0]
    km = km_ref[0]
    s = q.shape[0]
    nb = km.shape[0]
    lane = _iota((1, LANES), 1)
    blk = _iota((nb, 1), 0)
    own = _iota((1, s), 1) // MOBA_BLOCK
    past = blk < own
    nt = (((1,), (1,)), ((), ()))
    for hh in range(2):
        qh = jnp.where((lane // ATT_HEAD_DIM) == hh, q, 0.0)
        gate = lax.dot_general(km, qh, nt, precision=HIGHEST, preferred_element_type=F32)
        gate = jnp.where(past, gate, -jnp.inf)
        cnt = jnp.zeros((nb, s), F32)
        for jp in range(nb):
            gj = gate[jp:jp + 1, :]
            cnt = cnt + jnp.where(jp < blk, jnp.where(gj >= gate, 1.0, 0.0),
                                  jnp.where(gj > gate, 1.0, 0.0))
        keep = jnp.logical_and(past, cnt < MOBA_TOPK)
        sel_bias = jnp.where(keep, 0.0, NEG_BIG)
        for i in range(s // MOBA_BLOCK):
            o_ref[0, 0, i, hh * nb:(hh + 1) * nb, :] = sel_bias[:, i * MOBA_BLOCK:(i + 1) * MOBA_BLOCK]


def _moba_select(qn, kmean):
    b, s, _ = qn.shape
    nb = s // MOBA_BLOCK
    n_pairs = ATT_WIDTH // LANES
    return pl.pallas_call(
        _moba_select_body,
        out_shape=jax.ShapeDtypeStruct((b, n_pairs, nb, 2 * nb, MOBA_BLOCK), F32),
        grid=(b, n_pairs),
        in_specs=[pl.BlockSpec((1, s, LANES), lambda bi, p: (bi, 0, p)),
                  pl.BlockSpec((1, nb, LANES), lambda bi, p: (bi, 0, p))],
        out_specs=pl.BlockSpec((1, 1, nb, 2 * nb, MOBA_BLOCK), lambda bi, p: (bi, p, 0, 0, 0)),
        compiler_params=_cparams("parallel", "parallel"),
        name="moba_select",
    )(qn, kmean)


def _moba_body(q_ref, k_ref, vt_ref, sel_ref, o_ref, bias_ref, s_ref, p_ref):
    i = pl.program_id(2)
    bs = MOBA_BLOCK
    qt = MOBA_QUERY_TILE
    q = q_ref[0]
    nb = vt_ref.shape[1]
    scale = ATT_HEAD_DIM ** -0.5 * LOG2_E
    lane = _iota((1, LANES), 1)
    nt = (((1,), (1,)), ((), ()))

    streams = []
    for hh in range(2):
        qh = jnp.where((lane // ATT_HEAD_DIM) == hh, q, 0.0)
        for jp in range(nb):
            bias_ref[hh, jp] = sel_ref[0, 0, 0, hh * nb + jp:hh * nb + jp + 1, :]
        qb_all = (qh * scale).astype(BF16)
        vrows = slice(hh * ATT_HEAD_DIM, (hh + 1) * ATT_HEAD_DIM)
        for qs in range(0, bs, qt):
            streams.append((hh, qs, qb_all[qs:qs + qt, :], vrows))
    n_streams = len(streams)

    def score_tiles(j):
        k_j = k_ref[0, pl.ds(pl.multiple_of(j * bs, bs), bs), :]
        return tuple(lax.dot_general(k_j, qb, nt, preferred_element_type=F32)
                     for (_, _, qb, _) in streams)

    def value_products(j, probs):
        return [jnp.dot(vt_ref[0, j, vrows, :], probs[si], preferred_element_type=F32)
                for si, (_, _, _, vrows) in enumerate(streams)]

    ms, ls, accs = [], [], []
    for si, s in enumerate(score_tiles(i)):
        qs = streams[si][1]
        causal = _iota((bs, qt), 0) <= qs + _iota((bs, qt), 1)
        s = jnp.where(causal, s, NEG_BIG)
        m = jnp.max(s, axis=0, keepdims=True)
        p = jnp.exp2(s - m)
        ms.append(m)
        ls.append(jnp.sum(p, axis=0, keepdims=True))
        accs.append(jnp.zeros((ATT_HEAD_DIM, qt), F32))
        p_ref[1, si] = p.astype(BF16)
    for si, s in enumerate(score_tiles(0)):
        s_ref[0, si] = s

    def handle(j, slot, state):
        ms, ls, accs = state
        for si, s in enumerate(score_tiles(jnp.minimum(j + 1, nb - 1))):
            s_ref[1 - slot, si] = s
        owed = value_products(jnp.where(j == 0, i, j - 1),
                              [p_ref[1 - slot, si] for si in range(n_streams)])
        ms_o, ls_o, accs_o = [], [], []
        for si, (hh, qs, _, _) in enumerate(streams):
            s = s_ref[slot, si] + bias_ref[hh, j, :, qs:qs + qt]
            m_new = jnp.maximum(ms[si], jnp.max(s, axis=0, keepdims=True))
            alpha = jnp.exp2(ms[si] - m_new)
            p = jnp.exp2(s - m_new)
            ms_o.append(m_new)
            ls_o.append(alpha * ls[si] + jnp.sum(p, axis=0, keepdims=True))
            accs_o.append(alpha * (accs[si] + owed[si]))
            p_ref[slot, si] = p.astype(BF16)
        return tuple(ms_o), tuple(ls_o), tuple(accs_o)

    def two_blocks(t, state):
        return handle(2 * t + 1, 1, handle(2 * t, 0, state))

    trips = (i + 1) // 2
    ms, ls, accs = lax.fori_loop(0, trips, two_blocks, (tuple(ms), tuple(ls), tuple(accs)))
    owed = value_products(jnp.where(trips == 0, i, 2 * trips - 1),
                          [p_ref[1, si] for si in range(n_streams)])
    outs = [(accs[si] + owed[si]) / ls[si] for si in range(n_streams)]
    per_head = bs // qt
    out_rows = [jnp.concatenate(outs[hh * per_head:(hh + 1) * per_head], axis=1) for hh in range(2)]
    o_ref[0] = jnp.concatenate(out_rows, axis=0).T


def _moba_attention(qn, kn, vt, sel):
    b, s, _ = qn.shape
    nb = s // MOBA_BLOCK
    n_pairs = ATT_WIDTH // LANES
    n_streams = 2 * MOBA_BLOCK // MOBA_QUERY_TILE
    return pl.pallas_call(
        _moba_body,
        out_shape=jax.ShapeDtypeStruct((b, s, ATT_WIDTH), F32),
        grid=(b, n_pairs, nb),
        in_specs=[pl.BlockSpec((1, MOBA_BLOCK, LANES), lambda bi, p, i: (bi, i, p)),
                  pl.BlockSpec((1, s, LANES), lambda bi, p, i: (bi, 0, p)),
                  pl.BlockSpec((1, nb, LANES, MOBA_BLOCK), lambda bi, p, i: (bi, 0, p, 0)),
                  pl.BlockSpec((1, 1, 1, 2 * nb, MOBA_BLOCK), lambda bi, p, i: (bi, p, i, 0, 0))],
        out_specs=pl.BlockSpec((1, MOBA_BLOCK, LANES), lambda bi, p, i: (bi, i, p)),
        scratch_shapes=[pltpu.VMEM((2, nb, 1, MOBA_BLOCK), F32),
                        pltpu.VMEM((2, n_streams, MOBA_BLOCK, MOBA_QUERY_TILE), F32),
                        pltpu.VMEM((2, n_streams, MOBA_BLOCK, MOBA_QUERY_TILE), BF16)],
        compiler_params=_cparams("parallel", "parallel", "arbitrary"),
        name="moba_attention",
    )(qn, kn, vt, sel)


def _pool_body(ts, u_ref, halo_ref, w_ref, sc_ref, o_ref, ext_ref):
    i = pl.program_id(1)
    u = u_ref[0]
    ext_ref[0:POOL_HALO, :] = jnp.where(i > 0, halo_ref[0], 0.0)
    ext_ref[POOL_HALO:POOL_HALO + ts, :] = u
    t = i * ts + _iota((ts, 1), 0)
    outs = []
    for g, win in enumerate(POOL_WINDOWS):
        cols = slice(g * POOL_GROUP_DIM, (g + 1) * POOL_GROUP_DIM)
        ug = u[:, cols]
        acc = ug
        for k in range(1, win):
            acc = acc + ext_ref[POOL_HALO - k:POOL_HALO - k + ts, cols]
        cnt = jnp.minimum(t + 1, win).astype(F32)
        pooled = acc / cnt - ug
        outs.append(jnp.dot(pooled.astype(BF16), w_ref[g], preferred_element_type=F32))
    o_ref[0] = jnp.concatenate(outs, axis=1) * sc_ref[...]


def _multiscale_pool(proj, w_pool, pool_scale, ts):
    b, s, f = proj.shape
    width = len(POOL_WINDOWS) * POOL_GROUP_DIM
    assert max(POOL_WINDOWS) <= POOL_HALO and f % width == 0
    col = f // width - 1
    halo_per_tile = ts // POOL_HALO
    return pl.pallas_call(
        functools.partial(_pool_body, ts),
        out_shape=jax.ShapeDtypeStruct((b, s, width), F32),
        grid=(b, s // ts),
        in_specs=[pl.BlockSpec((1, ts, width), lambda bi, i: (bi, i, col)),
                  pl.BlockSpec((1, POOL_HALO, width),
                               lambda bi, i: (bi, jnp.maximum(i * halo_per_tile - 1, 0), col)),
                  pl.BlockSpec(w_pool.shape, lambda bi, i: (0, 0, 0)),
                  pl.BlockSpec((1, width), lambda bi, i: (0, 0))],
        out_specs=pl.BlockSpec((1, ts, width), lambda bi, i: (bi, i, 0)),
        scratch_shapes=[pltpu.VMEM((POOL_HALO + ts, width), F32)],
        compiler_params=_cparams("parallel", "parallel"),
        name="multiscale_pool",
    )(proj, proj, w_pool, pool_scale.reshape(1, width))


def _xattn_body(h_ref, kv_ref, g_ref, wq_ref, qg_ref, kg_ref, wo_ref, o_ref):
    h = h_ref[0]
    d = h.shape[-1]
    kv = kv_ref[0]
    hn = _rms(h, g_ref[...]).astype(BF16)
    q = jnp.dot(hn, wq_ref[...], preferred_element_type=F32)
    scale = XA_HEAD_DIM ** -0.5
    nt = (((1,), (1,)), ((), ()))
    outs = []
    for hh in range(XA_HEADS):
        cols = slice(hh * XA_HEAD_DIM, (hh + 1) * XA_HEAD_DIM)
        qh = _rms(q[:, cols], qg_ref[...]).astype(BF16)
        kh = _rms(kv[:, cols], kg_ref[...]).astype(BF16)
        vh = kv[:, d + hh * XA_HEAD_DIM:d + (hh + 1) * XA_HEAD_DIM].astype(BF16)
        s = lax.dot_general(qh, kh, nt, preferred_element_type=F32) * scale
        m = jnp.max(s, axis=-1, keepdims=True)
        p = jnp.exp(s - m)
        p = p / jnp.sum(p, axis=-1, keepdims=True)
        outs.append(jnp.dot(p.astype(BF16), vh, preferred_element_type=F32))
    o = jnp.concatenate(outs, axis=1).astype(BF16)
    o_ref[0] = h + jnp.dot(o, wo_ref[...], preferred_element_type=F32)


def _cross_attention(h, kv, gain, wq, q_gain, k_gain, wo, ts):
    b, s, d = h.shape
    m = kv.shape[1]
    return pl.pallas_call(
        _xattn_body,
        out_shape=jax.ShapeDtypeStruct((b, s, d), F32),
        grid=(b, s // ts),
        in_specs=[pl.BlockSpec((1, ts, d), lambda bi, i: (bi, i, 0)),
                  pl.BlockSpec((1, m, 2 * d), lambda bi, i: (bi, 0, 0)),
                  pl.BlockSpec((1, d), lambda bi, i: (0, 0)),
                  pl.BlockSpec((d, d), lambda bi, i: (0, 0)),
                  pl.BlockSpec((1, XA_HEAD_DIM), lambda bi, i: (0, 0)),
                  pl.BlockSpec((1, XA_HEAD_DIM), lambda bi, i: (0, 0)),
                  pl.BlockSpec((d, d), lambda bi, i: (0, 0))],
        out_specs=pl.BlockSpec((1, ts, d), lambda bi, i: (bi, i, 0)),
        compiler_params=_cparams("parallel", "parallel"),
        name="memory_cross_attention",
    )(h, kv, gain.reshape(1, d), wq, q_gain.reshape(1, -1), k_gain.reshape(1, -1), wo)


def _swiglu_body(h_ref, g_ref, wg_ref, wu_ref, wd_ref, o_ref):
    h = h_ref[...]
    hn = _rms(h, g_ref[...]).astype(BF16)
    f = wg_ref.shape[1]
    acc = h
    for c0 in range(0, f, FF_CHUNK):
        cols = slice(c0, min(c0 + FF_CHUNK, f))
        gate = jnp.dot(hn, wg_ref[:, cols], preferred_element_type=F32)
        up = jnp.dot(hn, wu_ref[:, cols], preferred_element_type=F32)
        act = (_silu(gate) * up).astype(BF16)
        acc = acc + jnp.dot(act, wd_ref[cols, :], preferred_element_type=F32)
    o_ref[...] = acc


def _swiglu(h, gain, wg, wu, wd, tm):
    n, d = h.shape
    f = wg.shape[1]
    return pl.pallas_call(
        _swiglu_body,
        out_shape=jax.ShapeDtypeStruct((n, d), F32),
        grid=(n // tm,),
        in_specs=[pl.BlockSpec((tm, d), lambda i: (i, 0)),
                  pl.BlockSpec((1, d), lambda i: (0, 0)),
                  pl.BlockSpec((d, f), lambda i: (0, 0)),
                  pl.BlockSpec((d, f), lambda i: (0, 0)),
                  pl.BlockSpec((f, d), lambda i: (0, 0))],
        out_specs=pl.BlockSpec((tm, d), lambda i: (i, 0)),
        compiler_params=_cparams("parallel"),
        name="swiglu",
    )(h, gain.reshape(1, d), wg, wu, wd)


def _ssd_body(zx_ref, cw_ref, cb_ref, dtb_ref, alog_ref, dskip_ref, ng_ref, o_ref,
              prev_ref, state_ref):
    c = pl.program_id(1)
    L = SSD_CHUNK
    n_state = SSD_STATE
    pair_w = 2 * SSD_HEAD_DIM
    assert pair_w == LANES and n_state == LANES and L == LANES

    @pl.when(c == 0)
    def _():
        prev_ref[...] = jnp.zeros_like(prev_ref)
        state_ref[...] = jnp.zeros_like(state_ref)

    blk = zx_ref[0]
    z = blk[:, :SSD_D_INNER]
    xr = blk[:, SSD_D_INNER:SSD_D_INNER + SSD_CONV_DIM]
    dtr = blk[:, SSD_D_INNER + SSD_CONV_DIM:SSD_D_INNER + SSD_CONV_DIM + LANES]

    tail = prev_ref[...]
    head = xr[0:SUBLANES]
    row = _iota((SUBLANES, 1), 0)
    conv = xr * cw_ref[SSD_CONV - 1:SSD_CONV, :]
    conv_head = head * cw_ref[SSD_CONV - 1:SSD_CONV, :]
    for k in range(1, SSD_CONV):
        tap = cw_ref[SSD_CONV - 1 - k:SSD_CONV - k, :]
        conv = conv + pltpu.roll(xr, k, 0) * tap
        conv_head = conv_head + jnp.where(row < k, pltpu.roll(tail, k, 0),
                                          pltpu.roll(head, k, 0)) * tap
    conv = jnp.concatenate([conv_head, conv[SUBLANES:]], axis=0)
    prev_ref[...] = xr[L - SUBLANES:L]
    xa = _silu(conv + cb_ref[...])
    xs = xa[:, :SSD_D_INNER]
    bm = xa[:, SSD_D_INNER:SSD_D_INNER + SSD_GROUPS * n_state]
    cm = xa[:, SSD_D_INNER + SSD_GROUPS * n_state:]

    dtx = dtr + dtb_ref[...]
    dt = jnp.maximum(dtx, 0.0) + jnp.log(1.0 + jnp.exp(-jnp.abs(dtx)))
    da = dt * (-jnp.exp(alog_ref[...]))
    tri = jnp.where(_iota((L, L), 1) <= _iota((L, L), 0), 1.0, 0.0)
    acum = jnp.dot(tri, da, precision=HIGHEST, preferred_element_type=F32)
    acum_t = acum.T
    dt_t = dt.T
    shifted_t = acum_t - jnp.log(dt_t)
    to_end_t = jnp.exp(acum_t[:, L - 1:L] - acum_t)
    causal = _iota((L, L), 1) <= _iota((L, L), 0)
    lane = _iota((1, LANES), 1)
    low = lane < SSD_HEAD_DIM
    nt = (((1,), (1,)), ((), ()))

    heads_per_group = SSD_HEADS // SSD_GROUPS
    y_pairs = []
    for g in range(SSD_GROUPS):
        bg = bm[:, g * n_state:(g + 1) * n_state]
        cg = cm[:, g * n_state:(g + 1) * n_state].astype(BF16)
        bg_t = bg.T
        cb = lax.dot_general(cg, bg.astype(BF16), nt, preferred_element_type=F32)
        for pp in range(heads_per_group // 2):
            pidx = g * (heads_per_group // 2) + pp
            cols = slice(pidx * pair_w, (pidx + 1) * pair_w)
            x_pair = xs[:, cols].astype(BF16)
            st_pair = state_ref[:, cols]
            cs = jnp.dot(cg, st_pair.astype(BF16), preferred_element_type=F32)
            ys, upds, lasts = [], [], []
            for hh in range(2):
                h = 2 * pidx + hh
                bc = jnp.broadcast_to(acum[:, h:h + 1], (L, L))
                seg = bc - shifted_t[h:h + 1, :]
                dec = jnp.exp(jnp.where(causal, seg, -jnp.inf))
                mm = (cb * dec).astype(BF16)
                ebc = jnp.exp(bc)
                ys.append(jnp.dot(mm, x_pair, preferred_element_type=F32) + ebc * cs)
                wrow = dt_t[h:h + 1, :] * to_end_t[h:h + 1, :]
                upds.append(jnp.dot((bg_t * wrow).astype(BF16), x_pair,
                                    preferred_element_type=F32))
                lasts.append(ebc[L - 1:L, :])
            y_pairs.append(jnp.where(low, ys[0], ys[1]))
            state_ref[:, cols] = (st_pair * jnp.where(low, lasts[0], lasts[1])
                                  + jnp.where(low, upds[0], upds[1]))

    y = jnp.concatenate(y_pairs, axis=1)
    y = (y + dskip_ref[...] * xs) * _silu(z)
    gw = SSD_D_INNER // SSD_GROUPS
    outs = []
    for g in range(SSD_GROUPS):
        cols = slice(g * gw, (g + 1) * gw)
        outs.append(_rms(y[:, cols], ng_ref[:, cols]))
    o_ref[0] = jnp.concatenate(outs, axis=1)


def _ssd_mixer(zx, conv_w, conv_b, dt_bias, a_log, d_skip, norm_g):
    b, s, f = zx.shape
    pad = LANES - SSD_HEADS
    dtb = jnp.pad(dt_bias, (0, pad)).reshape(1, LANES)
    alog = jnp.pad(a_log, (0, pad)).reshape(1, LANES)
    dskip = jnp.repeat(d_skip, SSD_HEAD_DIM).reshape(1, SSD_D_INNER)
    full = lambda shape: pl.BlockSpec(shape, lambda bi, c: (0,) * len(shape))
    return pl.pallas_call(
        _ssd_body,
        out_shape=jax.ShapeDtypeStruct((b, s, SSD_D_INNER), F32),
        grid=(b, s // SSD_CHUNK),
        in_specs=[pl.BlockSpec((1, SSD_CHUNK, f), lambda bi, c: (bi, c, 0)),
                  full((SSD_CONV, SSD_CONV_DIM)), full((1, SSD_CONV_DIM)),
                  full((1, LANES)), full((1, LANES)),
                  full((1, SSD_D_INNER)), full((1, SSD_D_INNER))],
        out_specs=pl.BlockSpec((1, SSD_CHUNK, SSD_D_INNER), lambda bi, c: (bi, c, 0)),
        scratch_shapes=[pltpu.VMEM((SUBLANES, SSD_CONV_DIM), F32),
                        pltpu.VMEM((SSD_STATE, SSD_D_INNER), F32)],
        compiler_params=_cparams("parallel", "arbitrary"),
        name="ssd_mixer",
    )(zx, conv_w, conv_b.reshape(1, -1), dtb, alog, dskip, norm_g.reshape(1, -1))


def _router_body(h_ref, g_ref, wr_ref, hn_ref, lg_ref):
    hn = _rms(h_ref[...], g_ref[...])
    _matrix_to_rows(hn_ref, hn)
    lg_ref[...] = jnp.dot(hn, wr_ref[...], precision=HIGHEST, preferred_element_type=F32)


def _router(h, gain, w_router_pad, tm):
    n, d = h.shape
    assert d == SUBLANES * LANES
    hn_rows, logits = pl.pallas_call(
        _router_body,
        out_shape=[jax.ShapeDtypeStruct((n * SUBLANES, LANES), F32),
                   jax.ShapeDtypeStruct((n, LANES), F32)],
        grid=(n // tm,),
        in_specs=[pl.BlockSpec((tm, d), lambda i: (i, 0)),
                  pl.BlockSpec((1, d), lambda i: (0, 0)),
                  pl.BlockSpec((d, LANES), lambda i: (0, 0))],
        out_specs=[pl.BlockSpec((tm * SUBLANES, LANES), lambda i: (i, 0)),
                   pl.BlockSpec((tm, LANES), lambda i: (i, 0))],
        compiler_params=_cparams("parallel"),
        name="moe_router",
    )(h, gain.reshape(1, d), w_router_pad)
    return hn_rows.reshape(n, SUBLANES, LANES), logits


def _matrix_to_rows(rows_ref, x):
    n = x.shape[0]
    for j in range(x.shape[1] // LANES):
        rows_ref[pl.ds(j, n, stride=SUBLANES), :] = x[:, j * LANES:(j + 1) * LANES]


def _row_copy(src_hbm, dst_ref, sems, slot, idx, r):
    rows = pl.ds(pl.multiple_of(r * SUBLANES, SUBLANES), SUBLANES)
    return pltpu.make_async_copy(src_hbm.at[idx], dst_ref.at[slot, rows, :], sems.at[slot])


def _start_row_gather(idx_ref, src_hbm, dst_ref, sems, slot, n_rows):
    def body(r, carry):
        _row_copy(src_hbm, dst_ref, sems, slot, idx_ref[0, 0, r], r).start()
        return carry

    lax.fori_loop(0, n_rows, body, 0, unroll=8)


def _wait_row_gather(src_hbm, dst_ref, sems, slot, n_rows):
    def body(r, carry):
        _row_copy(src_hbm, dst_ref, sems, slot, 0, r).wait()
        return carry

    lax.fori_loop(0, n_rows, body, 0, unroll=8)


def _rows_to_matrix(buf_ref, slot, first_row, n_rows, d):
    chunks = [buf_ref[slot, pl.ds(first_row * SUBLANES + j, n_rows, stride=SUBLANES), :]
              for j in range(d // LANES)]
    return jnp.concatenate(chunks, axis=1)


def _expert_body(blk_e_ref, n_valid_ref, tok_ref, tok_next_ref, hn_hbm, gate_ref,
                 wg_ref, wu_ref, wd_ref, o_ref, x_ref, sems):
    i = pl.program_id(0)
    last = pl.num_programs(0) - 1
    rows = gate_ref.shape[0]
    d = wg_ref.shape[1]
    slot = i % 2

    @pl.when(jnp.logical_and(i == 0, n_valid_ref[0] > 0))
    def _():
        _start_row_gather(tok_ref, hn_hbm, x_ref, sems, 0, rows)

    @pl.when(jnp.logical_and(i < last, n_valid_ref[jnp.minimum(i + 1, last)] > 0))
    def _():
        _start_row_gather(tok_next_ref, hn_hbm, x_ref, sems, 1 - slot, rows)

    @pl.when(n_valid_ref[i] > 0)
    def _():
        _wait_row_gather(hn_hbm, x_ref, sems, slot, rows)
        x = _rows_to_matrix(x_ref, slot, 0, rows, d).astype(BF16)
        f = wg_ref.shape[2]
        acc = jnp.zeros((rows, d), F32)
        for c0 in range(0, f, FF_CHUNK):
            cols = slice(c0, min(c0 + FF_CHUNK, f))
            gate = jnp.dot(x, wg_ref[0, :, cols], preferred_element_type=F32)
            up = jnp.dot(x, wu_ref[0, :, cols], preferred_element_type=F32)
            act = (_silu(gate) * up).astype(BF16)
            acc = acc + jnp.dot(act, wd_ref[0, cols, :], preferred_element_type=F32)
        _matrix_to_rows(o_ref, acc * gate_ref[...])

    @pl.when(n_valid_ref[i] == 0)
    def _():
        o_ref[...] = jnp.zeros_like(o_ref)


def _expert_ffn(blk_e, n_valid, row_tok, hn_rows, row_gate, wg, wu, wd):
    n_blocks = blk_e.shape[0]
    d = wg.shape[1]
    f = wg.shape[2]
    rows = MOE_ROWS
    tok = row_tok.reshape(n_blocks, 1, rows)
    grid_spec = pltpu.PrefetchScalarGridSpec(
        num_scalar_prefetch=2,
        grid=(n_blocks,),
        in_specs=[pl.BlockSpec((1, 1, rows), lambda i, be, nv: (i, 0, 0), memory_space=pltpu.SMEM),
                  pl.BlockSpec((1, 1, rows), lambda i, be, nv: (jnp.minimum(i + 1, n_blocks - 1), 0, 0),
                               memory_space=pltpu.SMEM),
                  pl.BlockSpec(memory_space=pl.ANY),
                  pl.BlockSpec((rows, 1), lambda i, be, nv: (i, 0)),
                  pl.BlockSpec((1, d, f), lambda i, be, nv: (be[i], 0, 0)),
                  pl.BlockSpec((1, d, f), lambda i, be, nv: (be[i], 0, 0)),
                  pl.BlockSpec((1, f, d), lambda i, be, nv: (be[i], 0, 0))],
        out_specs=pl.BlockSpec((rows * SUBLANES, LANES), lambda i, be, nv: (i, 0)),
        scratch_shapes=[pltpu.VMEM((2, rows * SUBLANES, LANES), F32),
                        pltpu.SemaphoreType.DMA((2,))],
    )
    y_rows = pl.pallas_call(
        _expert_body,
        out_shape=jax.ShapeDtypeStruct((n_blocks * rows * SUBLANES, LANES), F32),
        grid_spec=grid_spec,
        compiler_params=_cparams("arbitrary"),
        name="moe_expert_ffn",
    )(blk_e, n_valid, tok, tok, hn_rows, row_gate.reshape(-1, 1), wg, wu, wd)
    return y_rows.reshape(n_blocks * rows, SUBLANES, LANES)


def _combine_body(pos_ref, pos_next_ref, h_ref, y_hbm, o_ref, y_ref, sems):
    i = pl.program_id(0)
    last = pl.num_programs(0) - 1
    tc, d = o_ref.shape
    slot = i % 2

    @pl.when(i == 0)
    def _():
        _start_row_gather(pos_ref, y_hbm, y_ref, sems, 0, TOP_K * tc)

    @pl.when(i < last)
    def _():
        _start_row_gather(pos_next_ref, y_hbm, y_ref, sems, 1 - slot, TOP_K * tc)

    _wait_row_gather(y_hbm, y_ref, sems, slot, TOP_K * tc)
    acc = h_ref[...]
    for k in range(TOP_K):
        acc = acc + _rows_to_matrix(y_ref, slot, k * tc, tc, d)
    o_ref[...] = acc


def _moe_combine(h, pos, y_rows):
    n, d = h.shape
    tc = COMBINE_TOKENS
    n_tiles = n // tc
    pos_tiles = pos.reshape(n_tiles, tc, TOP_K).transpose(0, 2, 1).reshape(n_tiles, 1, TOP_K * tc)
    return pl.pallas_call(
        _combine_body,
        out_shape=jax.ShapeDtypeStruct((n, d), F32),
        grid=(n_tiles,),
        in_specs=[pl.BlockSpec((1, 1, TOP_K * tc), lambda i: (i, 0, 0), memory_space=pltpu.SMEM),
                  pl.BlockSpec((1, 1, TOP_K * tc), lambda i: (jnp.minimum(i + 1, n_tiles - 1), 0, 0),
                               memory_space=pltpu.SMEM),
                  pl.BlockSpec((tc, d), lambda i: (i, 0)),
                  pl.BlockSpec(memory_space=pl.ANY)],
        out_specs=pl.BlockSpec((tc, d), lambda i: (i, 0)),
        scratch_shapes=[pltpu.VMEM((2, TOP_K * tc * SUBLANES, LANES), F32),
                        pltpu.SemaphoreType.DMA((2,))],
        compiler_params=_cparams("arbitrary"),
        name="moe_combine",
    )(pos_tiles, pos_tiles, h, y_rows)


def _moe_layer(h, gain, w_router, wg, wu, wd, tm):
    n, d = h.shape
    e = w_router.shape[1]
    rows = MOE_ROWS
    n_slots = n * TOP_K
    hn_rows, logits = _router(h, gain, jnp.pad(w_router, ((0, 0), (0, LANES - e))), tm)
    top_logit, top_e = lax.top_k(logits[:, :e], TOP_K)
    gates = jax.nn.softmax(top_logit, axis=-1)

    e_flat = top_e.reshape(-1).astype(jnp.int32)
    slot_ids = jnp.arange(n_slots, dtype=jnp.int32)
    e_sorted, order = lax.sort((e_flat, slot_ids), num_keys=1, is_stable=True)
    experts = jnp.arange(e, dtype=jnp.int32)
    sorted_onehot = e_sorted[:, None] == experts[None, :]
    counts = jnp.sum(sorted_onehot, axis=0, dtype=jnp.int32)
    start = jnp.cumsum(counts) - counts
    padded = (counts + rows - 1) // rows * rows
    pad_end = jnp.cumsum(padded)
    pad_start = pad_end - padded
    shift = jnp.sum(jnp.where(sorted_onehot, (pad_start - start)[None, :], 0), axis=1)
    dest = slot_ids + shift
    _, pos = lax.sort((order, dest), num_keys=1)
    n_blocks = n_slots // rows + e
    n_rows = n_blocks * rows
    blk_start = jnp.arange(n_blocks, dtype=jnp.int32) * rows
    blk_e = jnp.minimum(jnp.sum(blk_start[:, None] >= pad_end[None, :], axis=1),
                        e - 1).astype(jnp.int32)
    n_valid = jnp.clip(pad_start[blk_e] + counts[blk_e] - blk_start, 0, rows).astype(jnp.int32)
    row_rank = (jnp.arange(rows, dtype=jnp.int32)[None, :] + (blk_start - pad_start[blk_e])[:, None])
    row_ok = row_rank < counts[blk_e][:, None]
    src = jnp.where(row_ok, start[blk_e][:, None] + row_rank, 0).reshape(-1)
    row_slot = order[src]
    row_tok = jnp.where(row_ok.reshape(-1), row_slot // TOP_K, 0)
    row_gate = jnp.where(row_ok.reshape(-1), gates.reshape(-1)[row_slot], 0.0)

    y_rows = _expert_ffn(blk_e, n_valid, row_tok, hn_rows, row_gate, wg, wu, wd)
    return _moe_combine(h, pos, y_rows)


def _rope_tables(positions):
    half = ATT_HEAD_DIM // 2
    inv_freq = ROPE_THETA ** (-jnp.arange(half, dtype=F32) / half)
    ang = positions.astype(F32)[:, None] * inv_freq[None, :]
    cos = jnp.cos(ang)
    sin = jnp.sin(ang)
    reps = LANES // ATT_HEAD_DIM
    return (jnp.tile(cos, (1, 2 * reps)), jnp.tile(jnp.concatenate([-sin, sin], axis=1), (1, reps)))


def kernel(x, mem, positions, mix_norm, xa_norm, mem_norm, ffn_norm, xa_wq, xa_wkv, xa_q_norm, xa_k_norm, xa_wo, hy_w_in, hy_q_norm, hy_k_norm, pool_w, pool_scale, hy_w_out, ffn_w_gate, ffn_w_up, ffn_w_down, ssd_w_in, ssd_conv_w, ssd_conv_b, ssd_dt_bias, ssd_a_log, ssd_d, ssd_norm, ssd_w_out, moe_router, moe_w_gate, moe_w_up, moe_w_down):
    b, s, d = x.shape
    n = b * s
    n_mem = mem.shape[1]
    depth = mix_norm.shape[0]
    tm = min(512, n)
    ts = min(512, s)
    bf = lambda w: w.astype(BF16)
    cos, sin = _rope_tables(positions)
    reps = LANES // ATT_HEAD_DIM

    h = x.reshape(n, d)
    mem2 = mem.reshape(b * n_mem, d)
    for layer in range(depth):
        j = layer // 2
        if layer % 2 == 0:
            proj = _norm_matmul(h, mix_norm[layer], bf(hy_w_in[j]), min(1024, n), 2048).reshape(b, s, -1)
            qn, kn, vt, kmean = _qkv_prep(proj, cos, sin,
                                          jnp.tile(hy_q_norm[j], reps).reshape(1, LANES),
                                          jnp.tile(hy_k_norm[j], reps).reshape(1, LANES))
            att = _moba_attention(qn, kn, vt, _moba_select(qn, kmean))
            pooled = _multiscale_pool(proj, bf(pool_w[j]), pool_scale[j], ts)
            w_out = bf(hy_w_out[j])
            h = _matmul_residual(h, [(att.reshape(n, -1), w_out[:ATT_WIDTH]),
                                     (pooled.reshape(n, -1), w_out[ATT_WIDTH:])], tm)
        else:
            w_in = ssd_w_in[j]
            w_cat = jnp.pad(w_in, ((0, 0), (0, SSD_DT_PAD - SSD_HEADS)))
            zx = _norm_matmul(h, mix_norm[layer], bf(w_cat), min(1024, n), SSD_PROJ_TILE).reshape(b, s, -1)
            y = _ssd_mixer(zx, ssd_conv_w[j], ssd_conv_b[j], ssd_dt_bias[j], ssd_a_log[j],
                           ssd_d[j], ssd_norm[j])
            h = _matmul_residual(h, [(y.reshape(n, -1), bf(ssd_w_out[j]))], tm)

        kv = _norm_matmul(mem2, mem_norm[layer], bf(xa_wkv[layer]), min(512, b * n_mem), 1024)
        h = _cross_attention(h.reshape(b, s, d), kv.reshape(b, n_mem, 2 * d), xa_norm[layer],
                             bf(xa_wq[layer]), xa_q_norm[layer], xa_k_norm[layer],
                             bf(xa_wo[layer]), ts).reshape(n, d)

        if layer % 2 == 0:
            h = _swiglu(h, ffn_norm[layer], bf(ffn_w_gate[j]), bf(ffn_w_up[j]),
                        bf(ffn_w_down[j]), tm)
        else:
            h = _moe_layer(h, ffn_norm[layer], moe_router[j], bf(moe_w_gate[j]),
                           bf(moe_w_up[j]), bf(moe_w_down[j]), tm)
    return h.reshape(b, s, d)
```

```python
import functools
import math

import jax
import jax.numpy as jnp
from jax import lax
from jax.experimental import pallas as pl
from jax.experimental.pallas import tpu as pltpu

F32 = jnp.float32
BF16 = jnp.bfloat16
HIGHEST = lax.Precision.HIGHEST

EPS = 1e-6
ROPE_THETA = 10000.0
LANES = 128
SUBLANES = 8
VMEM_LIMIT_BYTES = 56 * 1024 * 1024

ATT_HEADS = 8
ATT_HEAD_DIM = 64
ATT_WIDTH = ATT_HEADS * ATT_HEAD_DIM
POOL_WINDOWS = (2, 4, 8, 16)
POOL_GROUP_DIM = 128
POOL_HALO = 16
MOBA_BLOCK = 256
MOBA_TOPK = 3
MOBA_QUERY_TILE = 128
PREP_BLOCKS_PER_STEP = 4
LOG2_E = 1.4426950408889634
NEG_BIG = -1e30

SSD_D_INNER = 2048
SSD_HEAD_DIM = 64
SSD_HEADS = 32
SSD_GROUPS = 4
SSD_STATE = 128
SSD_CONV = 4
SSD_CHUNK = 128
SSD_CONV_DIM = SSD_D_INNER + 2 * SSD_GROUPS * SSD_STATE
SSD_DT_PAD = 256
SSD_PROJ_TILE = 1792

XA_HEADS = 4
XA_HEAD_DIM = 256

N_EXPERTS = 8
TOP_K = 2
MOE_ROWS = 512
FF_CHUNK = 512
COMBINE_TOKENS = 256


def _cparams(*sem):
    return pltpu.CompilerParams(dimension_semantics=sem, vmem_limit_bytes=VMEM_LIMIT_BYTES)


def _rms(x, g):
    return x * lax.rsqrt(jnp.mean(x * x, axis=-1, keepdims=True) + EPS) * g


def _silu(x):
    return x / (1.0 + jnp.exp(-x))


def _iota(shape, dim):
    return lax.broadcasted_iota(jnp.int32, shape, dim)


def _norm_matmul_body(x_ref, g_ref, w_ref, o_ref, xn_ref):
    @pl.when(pl.program_id(1) == 0)
    def _():
        xn_ref[...] = _rms(x_ref[...], g_ref[...]).astype(BF16)

    o_ref[...] = jnp.dot(xn_ref[...], w_ref[...], preferred_element_type=F32)


def _norm_matmul(x, gain, w, tm, tn):
    n, d = x.shape
    f = w.shape[1]
    return pl.pallas_call(
        _norm_matmul_body,
        out_shape=jax.ShapeDtypeStruct((n, f), F32),
        grid=(n // tm, f // tn),
        in_specs=[pl.BlockSpec((tm, d), lambda i, j: (i, 0)),
                  pl.BlockSpec((1, d), lambda i, j: (0, 0)),
                  pl.BlockSpec((d, tn), lambda i, j: (0, j))],
        out_specs=pl.BlockSpec((tm, tn), lambda i, j: (i, j)),
        scratch_shapes=[pltpu.VMEM((tm, d), BF16)],
        compiler_params=_cparams("parallel", "arbitrary"),
        name="norm_matmul",
    )(x, gain.reshape(1, d), w)


def _matmul_residual_body(n_pairs, res_ref, *refs):
    o_ref = refs[-1]
    acc = res_ref[...]
    for p in range(n_pairs):
        acc = acc + jnp.dot(refs[2 * p][...], refs[2 * p + 1][...],
                            preferred_element_type=F32)
    o_ref[...] = acc


def _matmul_residual(res, pairs, tm):
    n, d = res.shape
    in_specs = [pl.BlockSpec((tm, d), lambda i: (i, 0))]
    args = [res]
    for a, w in pairs:
        in_specs.append(pl.BlockSpec((tm, a.shape[1]), lambda i: (i, 0)))
        in_specs.append(pl.BlockSpec(w.shape, lambda i: (0, 0)))
        args += [a, w]
    return pl.pallas_call(
        functools.partial(_matmul_residual_body, len(pairs)),
        out_shape=jax.ShapeDtypeStruct((n, d), F32),
        grid=(n // tm,),
        in_specs=in_specs,
        out_specs=pl.BlockSpec((tm, d), lambda i: (i, 0)),
        compiler_params=_cparams("parallel"),
        name="matmul_residual",
    )(*args)


def _qkv_prep_body(q_ref, k_ref, v_ref, cos_ref, sin_ref, qg_ref, kg_ref,
                   qo_ref, ko_ref, vt_ref, km_ref):
    lane = _iota((1, LANES), 1)
    first_half = (lane % ATT_HEAD_DIM) < (ATT_HEAD_DIM // 2)
    r = _iota((LANES, LANES), 0) // ATT_HEAD_DIM
    c = _iota((LANES, LANES), 1) // ATT_HEAD_DIM
    head_mean = jnp.where(r == c, 1.0 / ATT_HEAD_DIM, 0.0).astype(BF16)
    cos = cos_ref[...]
    sin = sin_ref[...]

    def prep(x, g):
        sq = x * x
        hi = sq.astype(BF16)
        lo = (sq - hi.astype(F32)).astype(BF16)
        ms = (jnp.dot(hi, head_mean, preferred_element_type=F32)
              + jnp.dot(lo, head_mean, preferred_element_type=F32))
        xn = x * lax.rsqrt(ms + EPS) * g
        half = ATT_HEAD_DIM // 2
        partner = jnp.where(first_half, pltpu.roll(xn, LANES - half, 1), pltpu.roll(xn, half, 1))
        return xn * cos + partner * sin

    qo_ref[0] = prep(q_ref[0], qg_ref[...])
    kk = prep(k_ref[0], kg_ref[...])
    ko_ref[0] = kk.astype(BF16)
    v = v_ref[0]
    for r0 in range(v.shape[0] // MOBA_BLOCK):
        rows = slice(r0 * MOBA_BLOCK, (r0 + 1) * MOBA_BLOCK)
        km_ref[0, r0] = jnp.mean(kk[rows], axis=0, keepdims=True)
        vt_ref[0, r0] = v[rows].T.astype(BF16)


def _qkv_prep(proj, cos, sin, qg, kg):
    b, s, _ = proj.shape
    nb = s // MOBA_BLOCK
    n_pairs = ATT_WIDTH // LANES
    per_step = min(PREP_BLOCKS_PER_STEP, nb)
    rows = per_step * MOBA_BLOCK
    blk = (1, rows, LANES)
    qn, kn, vt, km = pl.pallas_call(
        _qkv_prep_body,
        out_shape=[jax.ShapeDtypeStruct((b, s, ATT_WIDTH), F32),
                   jax.ShapeDtypeStruct((b, s, ATT_WIDTH), BF16),
                   jax.ShapeDtypeStruct((b, nb, ATT_WIDTH, MOBA_BLOCK), BF16),
                   jax.ShapeDtypeStruct((b, nb, 1, ATT_WIDTH), F32)],
        grid=(b, nb // per_step, n_pairs),
        in_specs=[pl.BlockSpec(blk, lambda bi, i, p: (bi, i, p)),
                  pl.BlockSpec(blk, lambda bi, i, p: (bi, i, n_pairs + p)),
                  pl.BlockSpec(blk, lambda bi, i, p: (bi, i, 2 * n_pairs + p)),
                  pl.BlockSpec((rows, LANES), lambda bi, i, p: (i, 0)),
                  pl.BlockSpec((rows, LANES), lambda bi, i, p: (i, 0)),
                  pl.BlockSpec((1, LANES), lambda bi, i, p: (0, 0)),
                  pl.BlockSpec((1, LANES), lambda bi, i, p: (0, 0))],
        out_specs=[pl.BlockSpec(blk, lambda bi, i, p: (bi, i, p)),
                   pl.BlockSpec(blk, lambda bi, i, p: (bi, i, p)),
                   pl.BlockSpec((1, per_step, LANES, MOBA_BLOCK), lambda bi, i, p: (bi, i, p, 0)),
                   pl.BlockSpec((1, per_step, 1, LANES), lambda bi, i, p: (bi, i, 0, p))],
        compiler_params=_cparams("parallel", "parallel", "parallel"),
        name="moba_qkv_prep",
    )(proj, proj, proj, cos, sin, qg, kg)
    return qn, kn, vt, km.reshape(b, nb, ATT_WIDTH)


def _moba_select_body(q_ref, km_ref, o_ref):
    q = q_ref[0]
    km = km_ref[0]
    s = q.shape[0]
    nb = km.shape[0]
    lane = _iota((1, LANES), 1)
    blk = _iota((nb, 1), 0)
    own = _iota((1, s), 1) // MOBA_BLOCK
    past = blk < own
    nt = (((1,), (1,)), ((), ()))
    for hh in range(2):
        qh = jnp.where((lane // ATT_HEAD_DIM) == hh, q, 0.0)
        gate = lax.dot_general(km, qh, nt, precision=HIGHEST, preferred_element_type=F32)
        gate = jnp.where(past, gate, -jnp.inf)
        cnt = jnp.zeros((nb, s), F32)
        for jp in range(nb):
            gj = gate[jp:jp + 1, :]
            cnt = cnt + jnp.where(jp < blk, jnp.where(gj >= gate, 1.0, 0.0),
                                  jnp.where(gj > gate, 1.0, 0.0))
        keep = jnp.logical_and(past, cnt < MOBA_TOPK)
        sel_bias = jnp.where(keep, 0.0, NEG_BIG)
        for i in range(s // MOBA_BLOCK):
            o_ref[0, 0, i, hh * nb:(hh + 1) * nb, :] = sel_bias[:, i * MOBA_BLOCK:(i + 1) * MOBA_BLOCK]


def _moba_select(qn, kmean):
    b, s, _ = qn.shape
    nb = s // MOBA_BLOCK
    n_pairs = ATT_WIDTH // LANES
    return pl.pallas_call(
        _moba_select_body,
        out_shape=jax.ShapeDtypeStruct((b, n_pairs, nb, 2 * nb, MOBA_BLOCK), F32),
        grid=(b, n_pairs),
        in_specs=[pl.BlockSpec((1, s, LANES), lambda bi, p: (bi, 0, p)),
                  pl.BlockSpec((1, nb, LANES), lambda bi, p: (bi, 0, p))],
        out_specs=pl.BlockSpec((1, 1, nb, 2 * nb, MOBA_BLOCK), lambda bi, p: (bi, p, 0, 0, 0)),
        compiler_params=_cparams("parallel", "parallel"),
        name="moba_select",
    )(qn, kmean)


def _moba_body(q_ref, k_ref, vt_ref, sel_ref, o_ref, bias_ref, s_ref, p_ref):
    i = pl.program_id(2)
    bs = MOBA_BLOCK
    qt = MOBA_QUERY_TILE
    q = q_ref[0]
    nb = vt_ref.shape[1]
    scale = ATT_HEAD_DIM ** -0.5 * LOG2_E
    lane = _iota((1, LANES), 1)
    nt = (((1,), (1,)), ((), ()))

    streams = []
    for hh in range(2):
        qh = jnp.where((lane // ATT_HEAD_DIM) == hh, q, 0.0)
        for jp in range(nb):
            bias_ref[hh, jp] = sel_ref[0, 0, 0, hh * nb + jp:hh * nb + jp + 1, :]
        qb_all = (qh * scale).astype(BF16)
        vrows = slice(hh * ATT_HEAD_DIM, (hh + 1) * ATT_HEAD_DIM)
        for qs in range(0, bs, qt):
            streams.append((hh, qs, qb_all[qs:qs + qt, :], vrows))
    n_streams = len(streams)

    def score_tiles(j):
        k_j = k_ref[0, pl.ds(pl.multiple_of(j * bs, bs), bs), :]
        return tuple(lax.dot_general(k_j, qb, nt, preferred_element_type=F32)
                     for (_, _, qb, _) in streams)

    def value_products(j, probs):
        return [jnp.dot(vt_ref[0, j, vrows, :], probs[si], preferred_element_type=F32)
                for si, (_, _, _, vrows) in enumerate(streams)]

    ms, ls, accs = [], [], []
    for si, s in enumerate(score_tiles(i)):
        qs = streams[si][1]
        causal = _iota((bs, qt), 0) <= qs + _iota((bs, qt), 1)
        s = jnp.where(causal, s, NEG_BIG)
        m = jnp.max(s, axis=0, keepdims=True)
        p = jnp.exp2(s - m)
        ms.append(m)
        ls.append(jnp.sum(p, axis=0, keepdims=True))
        accs.append(jnp.zeros((ATT_HEAD_DIM, qt), F32))
        p_ref[1, si] = p.astype(BF16)
    for si, s in enumerate(score_tiles(0)):
        s_ref[0, si] = s

    def handle(j, slot, state):
        ms, ls, accs = state
        for si, s in enumerate(score_tiles(jnp.minimum(j + 1, nb - 1))):
            s_ref[1 - slot, si] = s
        owed = value_products(jnp.where(j == 0, i, j - 1),
                              [p_ref[1 - slot, si] for si in range(n_streams)])
        ms_o, ls_o, accs_o = [], [], []
        for si, (hh, qs, _, _) in enumerate(streams):
            s = s_ref[slot, si] + bias_ref[hh, j, :, qs:qs + qt]
            m_new = jnp.maximum(ms[si], jnp.max(s, axis=0, keepdims=True))
            alpha = jnp.exp2(ms[si] - m_new)
            p = jnp.exp2(s - m_new)
            ms_o.append(m_new)
            ls_o.append(alpha * ls[si] + jnp.sum(p, axis=0, keepdims=True))
            accs_o.append(alpha * (accs[si] + owed[si]))
            p_ref[slot, si] = p.astype(BF16)
        return tuple(ms_o), tuple(ls_o), tuple(accs_o)

    def two_blocks(t, state):
        return handle(2 * t + 1, 1, handle(2 * t, 0, state))

    trips = (i + 1) // 2
    ms, ls, accs = lax.fori_loop(0, trips, two_blocks, (tuple(ms), tuple(ls), tuple(accs)))
    owed = value_products(jnp.where(trips == 0, i, 2 * trips - 1),
                          [p_ref[1, si] for si in range(n_streams)])
    outs = [(accs[si] + owed[si]) / ls[si] for si in range(n_streams)]
    per_head = bs // qt
    out_rows = [jnp.concatenate(outs[hh * per_head:(hh + 1) * per_head], axis=1) for hh in range(2)]
    o_ref[0] = jnp.concatenate(out_rows, axis=0).T.astype(o_ref.dtype)


def _moba_attention(qn, kn, vt, sel):
    b, s, _ = qn.shape
    nb = s // MOBA_BLOCK
    n_pairs = ATT_WIDTH // LANES
    n_streams = 2 * MOBA_BLOCK // MOBA_QUERY_TILE
    return pl.pallas_call(
        _moba_body,
        out_shape=jax.ShapeDtypeStruct((b, s, ATT_WIDTH), BF16),
        grid=(b, n_pairs, nb),
        in_specs=[pl.BlockSpec((1, MOBA_BLOCK, LANES), lambda bi, p, i: (bi, i, p)),
                  pl.BlockSpec((1, s, LANES), lambda bi, p, i: (bi, 0, p)),
                  pl.BlockSpec((1, nb, LANES, MOBA_BLOCK), lambda bi, p, i: (bi, 0, p, 0)),
                  pl.BlockSpec((1, 1, 1, 2 * nb, MOBA_BLOCK), lambda bi, p, i: (bi, p, i, 0, 0))],
        out_specs=pl.BlockSpec((1, MOBA_BLOCK, LANES), lambda bi, p, i: (bi, i, p)),
        scratch_shapes=[pltpu.VMEM((2, nb, 1, MOBA_BLOCK), F32),
                        pltpu.VMEM((2, n_streams, MOBA_BLOCK, MOBA_QUERY_TILE), F32),
                        pltpu.VMEM((2, n_streams, MOBA_BLOCK, MOBA_QUERY_TILE), BF16)],
        compiler_params=_cparams("parallel", "parallel", "arbitrary"),
        name="moba_attention",
    )(qn, kn, vt, sel)


def _pool_body(ts, u_ref, halo_ref, w_ref, sc_ref, o_ref, ext_ref):
    i = pl.program_id(1)
    u = u_ref[0]
    ext_ref[0:POOL_HALO, :] = jnp.where(i > 0, halo_ref[0], 0.0)
    ext_ref[POOL_HALO:POOL_HALO + ts, :] = u
    t = i * ts + _iota((ts, 1), 0)
    outs = []
    for g, win in enumerate(POOL_WINDOWS):
        cols = slice(g * POOL_GROUP_DIM, (g + 1) * POOL_GROUP_DIM)
        ug = u[:, cols]
        acc = ug
        for k in range(1, win):
            acc = acc + ext_ref[POOL_HALO - k:POOL_HALO - k + ts, cols]
        cnt = jnp.minimum(t + 1, win).astype(F32)
        pooled = acc / cnt - ug
        outs.append(jnp.dot(pooled.astype(BF16), w_ref[g], preferred_element_type=F32))
    o_ref[0] = (jnp.concatenate(outs, axis=1) * sc_ref[...]).astype(o_ref.dtype)


def _multiscale_pool(proj, w_pool, pool_scale, ts):
    b, s, f = proj.shape
    width = len(POOL_WINDOWS) * POOL_GROUP_DIM
    assert max(POOL_WINDOWS) <= POOL_HALO and f % width == 0
    col = f // width - 1
    halo_per_tile = ts // POOL_HALO
    return pl.pallas_call(
        functools.partial(_pool_body, ts),
        out_shape=jax.ShapeDtypeStruct((b, s, width), BF16),
        grid=(b, s // ts),
        in_specs=[pl.BlockSpec((1, ts, width), lambda bi, i: (bi, i, col)),
                  pl.BlockSpec((1, POOL_HALO, width),
                               lambda bi, i: (bi, jnp.maximum(i * halo_per_tile - 1, 0), col)),
                  pl.BlockSpec(w_pool.shape, lambda bi, i: (0, 0, 0)),
                  pl.BlockSpec((1, width), lambda bi, i: (0, 0))],
        out_specs=pl.BlockSpec((1, ts, width), lambda bi, i: (bi, i, 0)),
        scratch_shapes=[pltpu.VMEM((POOL_HALO + ts, width), F32)],
        compiler_params=_cparams("parallel", "parallel"),
        name="multiscale_pool",
    )(proj, proj, w_pool, pool_scale.reshape(1, width))


def _xattn_body(with_router, h_ref, kv_ref, g_ref, wq_ref, qg_ref, kg_ref, wo_ref, *refs):
    o_ref = refs[2] if with_router else refs[0]
    h = h_ref[0]
    d = h.shape[-1]
    kv = kv_ref[0]
    hn = _rms(h, g_ref[...]).astype(BF16)
    q = jnp.dot(hn, wq_ref[...], preferred_element_type=F32)
    scale = XA_HEAD_DIM ** -0.5
    nt = (((1,), (1,)), ((), ()))
    outs = []
    for hh in range(XA_HEADS):
        cols = slice(hh * XA_HEAD_DIM, (hh + 1) * XA_HEAD_DIM)
        qh = _rms(q[:, cols], qg_ref[...]).astype(BF16)
        kh = _rms(kv[:, cols], kg_ref[...]).astype(BF16)
        vh = kv[:, d + hh * XA_HEAD_DIM:d + (hh + 1) * XA_HEAD_DIM].astype(BF16)
        s = lax.dot_general(qh, kh, nt, preferred_element_type=F32) * scale
        m = jnp.max(s, axis=-1, keepdims=True)
        p = jnp.exp(s - m)
        p = p / jnp.sum(p, axis=-1, keepdims=True)
        outs.append(jnp.dot(p.astype(BF16), vh, preferred_element_type=F32))
    o = jnp.concatenate(outs, axis=1).astype(BF16)
    h_out = h + jnp.dot(o, wo_ref[...], preferred_element_type=F32)
    o_ref[0] = h_out
    if with_router:
        rg_ref, wr_ref, _, hn_rows_ref, lg_ref = refs
        hn2 = _rms(h_out, rg_ref[...])
        _matrix_to_rows(hn_rows_ref, hn2)
        lg_ref[...] = jnp.dot(hn2, wr_ref[...], precision=HIGHEST, preferred_element_type=F32)


def _cross_attention(h, kv, gain, wq, q_gain, k_gain, wo, ts, router=None):
    b, s, d = h.shape
    m = kv.shape[1]
    steps = s // ts
    const = lambda shape: pl.BlockSpec(shape, lambda bi, i: (0,) * len(shape))
    in_specs = [pl.BlockSpec((1, ts, d), lambda bi, i: (bi, i, 0)),
                pl.BlockSpec((1, m, 2 * d), lambda bi, i: (bi, 0, 0)),
                const((1, d)), const((d, d)), const((1, XA_HEAD_DIM)), const((1, XA_HEAD_DIM)),
                const((d, d))]
    args = [h, kv, gain.reshape(1, d), wq, q_gain.reshape(1, -1), k_gain.reshape(1, -1), wo]
    out_shape = [jax.ShapeDtypeStruct((b, s, d), F32)]
    out_specs = [pl.BlockSpec((1, ts, d), lambda bi, i: (bi, i, 0))]
    if router is not None:
        assert d == SUBLANES * LANES
        ffn_gain, w_router_pad = router
        in_specs += [const((1, d)), const((d, LANES))]
        args += [ffn_gain.reshape(1, d), w_router_pad]
        out_shape += [jax.ShapeDtypeStruct((b * s * SUBLANES, LANES), F32),
                      jax.ShapeDtypeStruct((b * s, LANES), F32)]
        out_specs += [pl.BlockSpec((ts * SUBLANES, LANES), lambda bi, i: (bi * steps + i, 0)),
                      pl.BlockSpec((ts, LANES), lambda bi, i: (bi * steps + i, 0))]
    outs = pl.pallas_call(
        functools.partial(_xattn_body, router is not None),
        out_shape=out_shape,
        grid=(b, steps),
        in_specs=in_specs,
        out_specs=out_specs,
        compiler_params=_cparams("parallel", "parallel"),
        name="memory_cross_attention",
    )(*args)
    if router is None:
        return outs[0]
    return outs[0], outs[1].reshape(b * s, SUBLANES, LANES), outs[2]


def _swiglu_body(h_ref, g_ref, wg_ref, wu_ref, wd_ref, o_ref):
    h = h_ref[...]
    hn = _rms(h, g_ref[...]).astype(BF16)
    f = wg_ref.shape[1]
    acc = h
    for c0 in range(0, f, FF_CHUNK):
        cols = slice(c0, min(c0 + FF_CHUNK, f))
        gate = jnp.dot(hn, wg_ref[:, cols], preferred_element_type=F32)
        up = jnp.dot(hn, wu_ref[:, cols], preferred_element_type=F32)
        act = (_silu(gate) * up).astype(BF16)
        acc = acc + jnp.dot(act, wd_ref[cols, :], preferred_element_type=F32)
    o_ref[...] = acc


def _swiglu(h, gain, wg, wu, wd, tm):
    n, d = h.shape
    f = wg.shape[1]
    return pl.pallas_call(
        _swiglu_body,
        out_shape=jax.ShapeDtypeStruct((n, d), F32),
        grid=(n // tm,),
        in_specs=[pl.BlockSpec((tm, d), lambda i: (i, 0)),
                  pl.BlockSpec((1, d), lambda i: (0, 0)),
                  pl.BlockSpec((d, f), lambda i: (0, 0)),
                  pl.BlockSpec((d, f), lambda i: (0, 0)),
                  pl.BlockSpec((f, d), lambda i: (0, 0))],
        out_specs=pl.BlockSpec((tm, d), lambda i: (i, 0)),
        compiler_params=_cparams("parallel"),
        name="swiglu",
    )(h, gain.reshape(1, d), wg, wu, wd)


def _ssd_body(zx_ref, cw_ref, cb_ref, dtb_ref, alog_ref, dskip_ref, ng_ref, o_ref,
              prev_ref, state_ref):
    c = pl.program_id(1)
    L = SSD_CHUNK
    n_state = SSD_STATE
    pair_w = 2 * SSD_HEAD_DIM
    assert pair_w == LANES and n_state == LANES and L == LANES

    @pl.when(c == 0)
    def _():
        prev_ref[...] = jnp.zeros_like(prev_ref)
        state_ref[...] = jnp.zeros_like(state_ref)

    blk = zx_ref[0]
    z = blk[:, :SSD_D_INNER]
    xr = blk[:, SSD_D_INNER:SSD_D_INNER + SSD_CONV_DIM]
    dtr = blk[:, SSD_D_INNER + SSD_CONV_DIM:SSD_D_INNER + SSD_CONV_DIM + LANES]

    tail = prev_ref[...]
    head = xr[0:SUBLANES]
    row = _iota((SUBLANES, 1), 0)
    conv = xr * cw_ref[SSD_CONV - 1:SSD_CONV, :]
    conv_head = head * cw_ref[SSD_CONV - 1:SSD_CONV, :]
    for k in range(1, SSD_CONV):
        tap = cw_ref[SSD_CONV - 1 - k:SSD_CONV - k, :]
        conv = conv + pltpu.roll(xr, k, 0) * tap
        conv_head = conv_head + jnp.where(row < k, pltpu.roll(tail, k, 0),
                                          pltpu.roll(head, k, 0)) * tap
    conv = jnp.concatenate([conv_head, conv[SUBLANES:]], axis=0)
    prev_ref[...] = xr[L - SUBLANES:L]
    xa = _silu(conv + cb_ref[...])
    xs = xa[:, :SSD_D_INNER]
    bm = xa[:, SSD_D_INNER:SSD_D_INNER + SSD_GROUPS * n_state]
    cm = xa[:, SSD_D_INNER + SSD_GROUPS * n_state:]

    dtx = dtr + dtb_ref[...]
    dt = jnp.maximum(dtx, 0.0) + jnp.log(1.0 + jnp.exp(-jnp.abs(dtx)))
    da = dt * (-jnp.exp(alog_ref[...]))
    tri = jnp.where(_iota((L, L), 1) <= _iota((L, L), 0), 1.0, 0.0)
    acum = jnp.dot(tri, da, precision=HIGHEST, preferred_element_type=F32)
    acum_t = acum.T
    dt_t = dt.T
    shifted_t = acum_t - jnp.log(dt_t)
    to_end_t = jnp.exp(acum_t[:, L - 1:L] - acum_t)
    causal = _iota((L, L), 1) <= _iota((L, L), 0)
    lane = _iota((1, LANES), 1)
    low = lane < SSD_HEAD_DIM
    nt = (((1,), (1,)), ((), ()))

    heads_per_group = SSD_HEADS // SSD_GROUPS
    y_pairs = []
    for g in range(SSD_GROUPS):
        bg = bm[:, g * n_state:(g + 1) * n_state]
        cg = cm[:, g * n_state:(g + 1) * n_state].astype(BF16)
        bg_t = bg.T
        cb = lax.dot_general(cg, bg.astype(BF16), nt, preferred_element_type=F32)
        for pp in range(heads_per_group // 2):
            pidx = g * (heads_per_group // 2) + pp
            cols = slice(pidx * pair_w, (pidx + 1) * pair_w)
            x_pair = xs[:, cols].astype(BF16)
            st_pair = state_ref[:, cols]
            cs = jnp.dot(cg, st_pair.astype(BF16), preferred_element_type=F32)
            ys, upds, lasts = [], [], []
            for hh in range(2):
                h = 2 * pidx + hh
                bc = jnp.broadcast_to(acum[:, h:h + 1], (L, L))
                seg = bc - shifted_t[h:h + 1, :]
                dec = jnp.exp(jnp.where(causal, seg, -jnp.inf))
                mm = (cb * dec).astype(BF16)
                ebc = jnp.exp(bc)
                ys.append(jnp.dot(mm, x_pair, preferred_element_type=F32) + ebc * cs)
                wrow = dt_t[h:h + 1, :] * to_end_t[h:h + 1, :]
                upds.append(jnp.dot((bg_t * wrow).astype(BF16), x_pair,
                                    preferred_element_type=F32))
                lasts.append(ebc[L - 1:L, :])
            y_pairs.append(jnp.where(low, ys[0], ys[1]))
            state_ref[:, cols] = (st_pair * jnp.where(low, lasts[0], lasts[1])
                                  + jnp.where(low, upds[0], upds[1]))

    y = jnp.concatenate(y_pairs, axis=1)
    y = (y + dskip_ref[...] * xs) * _silu(z)
    gw = SSD_D_INNER // SSD_GROUPS
    outs = []
    for g in range(SSD_GROUPS):
        cols = slice(g * gw, (g + 1) * gw)
        outs.append(_rms(y[:, cols], ng_ref[:, cols]))
    o_ref[0] = jnp.concatenate(outs, axis=1).astype(o_ref.dtype)


def _ssd_mixer(zx, conv_w, conv_b, dt_bias, a_log, d_skip, norm_g):
    b, s, f = zx.shape
    pad = LANES - SSD_HEADS
    dtb = jnp.pad(dt_bias, (0, pad)).reshape(1, LANES)
    alog = jnp.pad(a_log, (0, pad)).reshape(1, LANES)
    dskip = jnp.repeat(d_skip, SSD_HEAD_DIM).reshape(1, SSD_D_INNER)
    full = lambda shape: pl.BlockSpec(shape, lambda bi, c: (0,) * len(shape))
    return pl.pallas_call(
        _ssd_body,
        out_shape=jax.ShapeDtypeStruct((b, s, SSD_D_INNER), BF16),
        grid=(b, s // SSD_CHUNK),
        in_specs=[pl.BlockSpec((1, SSD_CHUNK, f), lambda bi, c: (bi, c, 0)),
                  full((SSD_CONV, SSD_CONV_DIM)), full((1, SSD_CONV_DIM)),
                  full((1, LANES)), full((1, LANES)),
                  full((1, SSD_D_INNER)), full((1, SSD_D_INNER))],
        out_specs=pl.BlockSpec((1, SSD_CHUNK, SSD_D_INNER), lambda bi, c: (bi, c, 0)),
        scratch_shapes=[pltpu.VMEM((SUBLANES, SSD_CONV_DIM), F32),
                        pltpu.VMEM((SSD_STATE, SSD_D_INNER), F32)],
        compiler_params=_cparams("parallel", "arbitrary"),
        name="ssd_mixer",
    )(zx, conv_w, conv_b.reshape(1, -1), dtb, alog, dskip, norm_g.reshape(1, -1))


def _matrix_to_rows(rows_ref, x):
    n = x.shape[0]
    for j in range(x.shape[1] // LANES):
        rows_ref[pl.ds(j, n, stride=SUBLANES), :] = x[:, j * LANES:(j + 1) * LANES]


def _row_copy(src_hbm, dst_ref, sems, slot, idx, r):
    rows = pl.ds(pl.multiple_of(r * SUBLANES, SUBLANES), SUBLANES)
    return pltpu.make_async_copy(src_hbm.at[idx], dst_ref.at[slot, rows, :], sems.at[slot])


def _start_row_gather(idx_ref, src_hbm, dst_ref, sems, slot, n_rows):
    def body(r, carry):
        _row_copy(src_hbm, dst_ref, sems, slot, idx_ref[0, 0, r], r).start()
        return carry

    lax.fori_loop(0, n_rows, body, 0, unroll=8)


def _wait_row_gather(src_hbm, dst_ref, sems, slot, n_rows):
    def body(r, carry):
        _row_copy(src_hbm, dst_ref, sems, slot, 0, r).wait()
        return carry

    lax.fori_loop(0, n_rows, body, 0, unroll=8)


def _rows_to_matrix(buf_ref, slot, first_row, n_rows, d):
    chunks = [buf_ref[slot, pl.ds(first_row * SUBLANES + j, n_rows, stride=SUBLANES), :]
              for j in range(d // LANES)]
    return jnp.concatenate(chunks, axis=1)


def _expert_body(blk_e_ref, n_valid_ref, tok_ref, tok_next_ref, hn_hbm, gate_ref,
                 wg_ref, wu_ref, wd_ref, o_ref, x_ref, sems):
    i = pl.program_id(0)
    last = pl.num_programs(0) - 1
    rows = gate_ref.shape[0]
    d = wg_ref.shape[1]
    slot = i % 2

    @pl.when(jnp.logical_and(i == 0, n_valid_ref[0] > 0))
    def _():
        _start_row_gather(tok_ref, hn_hbm, x_ref, sems, 0, rows)

    @pl.when(jnp.logical_and(i < last, n_valid_ref[jnp.minimum(i + 1, last)] > 0))
    def _():
        _start_row_gather(tok_next_ref, hn_hbm, x_ref, sems, 1 - slot, rows)

    @pl.when(n_valid_ref[i] > 0)
    def _():
        _wait_row_gather(hn_hbm, x_ref, sems, slot, rows)
        x = _rows_to_matrix(x_ref, slot, 0, rows, d).astype(BF16)
        f = wg_ref.shape[2]
        acc = jnp.zeros((rows, d), F32)
        for c0 in range(0, f, FF_CHUNK):
            cols = slice(c0, min(c0 + FF_CHUNK, f))
            gate = jnp.dot(x, wg_ref[0, :, cols], preferred_element_type=F32)
            up = jnp.dot(x, wu_ref[0, :, cols], preferred_element_type=F32)
            act = (_silu(gate) * up).astype(BF16)
            acc = acc + jnp.dot(act, wd_ref[0, cols, :], preferred_element_type=F32)
        _matrix_to_rows(o_ref, acc * gate_ref[...])

    @pl.when(n_valid_ref[i] == 0)
    def _():
        o_ref[...] = jnp.zeros_like(o_ref)


def _expert_ffn(blk_e, n_valid, row_tok, hn_rows, row_gate, wg, wu, wd):
    n_blocks = blk_e.shape[0]
    d = wg.shape[1]
    f = wg.shape[2]
    rows = MOE_ROWS
    tok = row_tok.reshape(n_blocks, 1, rows)
    grid_spec = pltpu.PrefetchScalarGridSpec(
        num_scalar_prefetch=2,
        grid=(n_blocks,),
        in_specs=[pl.BlockSpec((1, 1, rows), lambda i, be, nv: (i, 0, 0), memory_space=pltpu.SMEM),
                  pl.BlockSpec((1, 1, rows), lambda i, be, nv: (jnp.minimum(i + 1, n_blocks - 1), 0, 0),
                               memory_space=pltpu.SMEM),
                  pl.BlockSpec(memory_space=pl.ANY),
                  pl.BlockSpec((rows, 1), lambda i, be, nv: (i, 0)),
                  pl.BlockSpec((1, d, f), lambda i, be, nv: (be[i], 0, 0)),
                  pl.BlockSpec((1, d, f), lambda i, be, nv: (be[i], 0, 0)),
                  pl.BlockSpec((1, f, d), lambda i, be, nv: (be[i], 0, 0))],
        out_specs=pl.BlockSpec((rows * SUBLANES, LANES), lambda i, be, nv: (i, 0)),
        scratch_shapes=[pltpu.VMEM((2, rows * SUBLANES, LANES), F32),
                        pltpu.SemaphoreType.DMA((2,))],
    )
    y_rows = pl.pallas_call(
        _expert_body,
        out_shape=jax.ShapeDtypeStruct((n_blocks * rows * SUBLANES, LANES), F32),
        grid_spec=grid_spec,
        compiler_params=_cparams("arbitrary"),
        name="moe_expert_ffn",
    )(blk_e, n_valid, tok, tok, hn_rows, row_gate.reshape(-1, 1), wg, wu, wd)
    return y_rows.reshape(n_blocks * rows, SUBLANES, LANES)


def _combine_body(pos_ref, pos_next_ref, h_ref, y_hbm, o_ref, y_ref, sems):
    i = pl.program_id(0)
    last = pl.num_programs(0) - 1
    tc, d = o_ref.shape
    slot = i % 2

    @pl.when(i == 0)
    def _():
        _start_row_gather(pos_ref, y_hbm, y_ref, sems, 0, TOP_K * tc)

    @pl.when(i < last)
    def _():
        _start_row_gather(pos_next_ref, y_hbm, y_ref, sems, 1 - slot, TOP_K * tc)

    _wait_row_gather(y_hbm, y_ref, sems, slot, TOP_K * tc)
    acc = h_ref[...]
    for k in range(TOP_K):
        acc = acc + _rows_to_matrix(y_ref, slot, k * tc, tc, d)
    o_ref[...] = acc


def _moe_combine(h, pos, y_rows):
    n, d = h.shape
    tc = COMBINE_TOKENS
    n_tiles = n // tc
    pos_tiles = pos.reshape(n_tiles, tc, TOP_K).transpose(0, 2, 1).reshape(n_tiles, 1, TOP_K * tc)
    return pl.pallas_call(
        _combine_body,
        out_shape=jax.ShapeDtypeStruct((n, d), F32),
        grid=(n_tiles,),
        in_specs=[pl.BlockSpec((1, 1, TOP_K * tc), lambda i: (i, 0, 0), memory_space=pltpu.SMEM),
                  pl.BlockSpec((1, 1, TOP_K * tc), lambda i: (jnp.minimum(i + 1, n_tiles - 1), 0, 0),
                               memory_space=pltpu.SMEM),
                  pl.BlockSpec((tc, d), lambda i: (i, 0)),
                  pl.BlockSpec(memory_space=pl.ANY)],
        out_specs=pl.BlockSpec((tc, d), lambda i: (i, 0)),
        scratch_shapes=[pltpu.VMEM((2, TOP_K * tc * SUBLANES, LANES), F32),
                        pltpu.SemaphoreType.DMA((2,))],
        compiler_params=_cparams("arbitrary"),
        name="moe_combine",
    )(pos_tiles, pos_tiles, h, y_rows)


def _moe_layer(h, hn_rows, logits, e, wg, wu, wd):
    n, d = h.shape
    rows = MOE_ROWS
    n_slots = n * TOP_K
    top_logit, top_e = lax.top_k(logits[:, :e], TOP_K)
    gates = jax.nn.softmax(top_logit, axis=-1)

    e_flat = top_e.reshape(-1).astype(jnp.int32)
    slot_ids = jnp.arange(n_slots, dtype=jnp.int32)
    e_sorted, order = lax.sort((e_flat, slot_ids), num_keys=1, is_stable=True)
    experts = jnp.arange(e, dtype=jnp.int32)
    sorted_onehot = e_sorted[:, None] == experts[None, :]
    counts = jnp.sum(sorted_onehot, axis=0, dtype=jnp.int32)
    start = jnp.cumsum(counts) - counts
    padded = (counts + rows - 1) // rows * rows
    pad_end = jnp.cumsum(padded)
    pad_start = pad_end - padded
    shift = jnp.sum(jnp.where(sorted_onehot, (pad_start - start)[None, :], 0), axis=1)
    dest = slot_ids + shift
    _, pos = lax.sort((order, dest), num_keys=1)
    n_blocks = n_slots // rows + e
    n_rows = n_blocks * rows
    blk_start = jnp.arange(n_blocks, dtype=jnp.int32) * rows
    blk_e = jnp.minimum(jnp.sum(blk_start[:, None] >= pad_end[None, :], axis=1),
                        e - 1).astype(jnp.int32)
    n_valid = jnp.clip(pad_start[blk_e] + counts[blk_e] - blk_start, 0, rows).astype(jnp.int32)
    row_rank = (jnp.arange(rows, dtype=jnp.int32)[None, :] + (blk_start - pad_start[blk_e])[:, None])
    row_ok = row_rank < counts[blk_e][:, None]
    src = jnp.where(row_ok, start[blk_e][:, None] + row_rank, 0).reshape(-1)
    row_slot = order[src]
    row_tok = jnp.where(row_ok.reshape(-1), row_slot // TOP_K, 0)
    row_gate = jnp.where(row_ok.reshape(-1), gates.reshape(-1)[row_slot], 0.0)

    y_rows = _expert_ffn(blk_e, n_valid, row_tok, hn_rows, row_gate, wg, wu, wd)
    return _moe_combine(h, pos, y_rows)


def _rope_tables(positions):
    half = ATT_HEAD_DIM // 2
    inv_freq = ROPE_THETA ** (-jnp.arange(half, dtype=F32) / half)
    ang = positions.astype(F32)[:, None] * inv_freq[None, :]
    cos = jnp.cos(ang)
    sin = jnp.sin(ang)
    reps = LANES // ATT_HEAD_DIM
    return (jnp.tile(cos, (1, 2 * reps)), jnp.tile(jnp.concatenate([-sin, sin], axis=1), (1, reps)))


def kernel(x, mem, positions, mix_norm, xa_norm, mem_norm, ffn_norm, xa_wq, xa_wkv, xa_q_norm, xa_k_norm, xa_wo, hy_w_in, hy_q_norm, hy_k_norm, pool_w, pool_scale, hy_w_out, ffn_w_gate, ffn_w_up, ffn_w_down, ssd_w_in, ssd_conv_w, ssd_conv_b, ssd_dt_bias, ssd_a_log, ssd_d, ssd_norm, ssd_w_out, moe_router, moe_w_gate, moe_w_up, moe_w_down):
    b, s, d = x.shape
    n = b * s
    n_mem = mem.shape[1]
    depth = mix_norm.shape[0]
    tm = min(512, n)
    ts = min(512, s)
    bf = lambda w: w.astype(BF16)
    cos, sin = _rope_tables(positions)
    reps = LANES // ATT_HEAD_DIM

    h = x.reshape(n, d)
    mem2 = mem.reshape(b * n_mem, d)
    for layer in range(depth):
        j = layer // 2
        if layer % 2 == 0:
            proj = _norm_matmul(h, mix_norm[layer], bf(hy_w_in[j]), min(1024, n), 2048).reshape(b, s, -1)
            qn, kn, vt, kmean = _qkv_prep(proj, cos, sin,
                                          jnp.tile(hy_q_norm[j], reps).reshape(1, LANES),
                                          jnp.tile(hy_k_norm[j], reps).reshape(1, LANES))
            att = _moba_attention(qn, kn, vt, _moba_select(qn, kmean))
            pooled = _multiscale_pool(proj, bf(pool_w[j]), pool_scale[j], ts)
            w_out = bf(hy_w_out[j])
            h = _matmul_residual(h, [(att.reshape(n, -1), w_out[:ATT_WIDTH]),
                                     (pooled.reshape(n, -1), w_out[ATT_WIDTH:])], tm)
        else:
            w_in = ssd_w_in[j]
            w_cat = jnp.pad(w_in, ((0, 0), (0, SSD_DT_PAD - SSD_HEADS)))
            zx = _norm_matmul(h, mix_norm[layer], bf(w_cat), min(1024, n), SSD_PROJ_TILE).reshape(b, s, -1)
            y = _ssd_mixer(zx, ssd_conv_w[j], ssd_conv_b[j], ssd_dt_bias[j], ssd_a_log[j],
                           ssd_d[j], ssd_norm[j])
            h = _matmul_residual(h, [(y.reshape(n, -1), bf(ssd_w_out[j]))], tm)

        kv = _norm_matmul(mem2, mem_norm[layer], bf(xa_wkv[layer]), min(512, b * n_mem), 1024)
        xa_args = (h.reshape(b, s, d), kv.reshape(b, n_mem, 2 * d), xa_norm[layer],
                   bf(xa_wq[layer]), xa_q_norm[layer], xa_k_norm[layer], bf(xa_wo[layer]), ts)
        if layer % 2 == 0:
            h = _cross_attention(*xa_args).reshape(n, d)
            h = _swiglu(h, ffn_norm[layer], bf(ffn_w_gate[j]), bf(ffn_w_up[j]),
                        bf(ffn_w_down[j]), tm)
        else:
            e = moe_router.shape[-1]
            w_router_pad = jnp.pad(moe_router[j], ((0, 0), (0, LANES - e)))
            h3, hn_rows, logits = _cross_attention(*xa_args, router=(ffn_norm[layer], w_router_pad))
            h = _moe_layer(h3.reshape(n, d), hn_rows, logits, e, bf(moe_w_gate[j]),
                           bf(moe_w_up[j]), bf(moe_w_down[j]))
    return h.reshape(b, s, d)
```

```python
import functools
import math

import jax
import jax.numpy as jnp
from jax import lax
from jax.experimental import pallas as pl
from jax.experimental.pallas import tpu as pltpu

F32 = jnp.float32
BF16 = jnp.bfloat16
HIGHEST = lax.Precision.HIGHEST

EPS = 1e-6
ROPE_THETA = 10000.0
LANES = 128
SUBLANES = 8
VMEM_LIMIT_BYTES = 56 * 1024 * 1024

ATT_HEADS = 8
ATT_HEAD_DIM = 64
ATT_WIDTH = ATT_HEADS * ATT_HEAD_DIM
POOL_WINDOWS = (2, 4, 8, 16)
POOL_GROUP_DIM = 128
POOL_HALO = 16
MOBA_BLOCK = 256
MOBA_TOPK = 3
MOBA_QUERY_TILE = 128
PREP_BLOCKS_PER_STEP = 4
LOG2_E = 1.4426950408889634
NEG_BIG = -1e30

SSD_D_INNER = 2048
SSD_HEAD_DIM = 64
SSD_HEADS = 32
SSD_GROUPS = 4
SSD_STATE = 128
SSD_CONV = 4
SSD_CHUNK = 128
SSD_CONV_DIM = SSD_D_INNER + 2 * SSD_GROUPS * SSD_STATE
SSD_DT_PAD = 256
SSD_PROJ_TILE = 1792

XA_HEADS = 4
XA_HEAD_DIM = 256

N_EXPERTS = 8
TOP_K = 2
MOE_ROWS = 512
FF_CHUNK = 512
COMBINE_TOKENS = 256


def _cparams(*sem):
    return pltpu.CompilerParams(dimension_semantics=sem, vmem_limit_bytes=VMEM_LIMIT_BYTES)


def _rms(x, g):
    return x * lax.rsqrt(jnp.mean(x * x, axis=-1, keepdims=True) + EPS) * g


def _silu(x):
    return x / (1.0 + jnp.exp(-x))


def _iota(shape, dim):
    return lax.broadcasted_iota(jnp.int32, shape, dim)


def _norm_matmul_body(x_ref, g_ref, w_ref, o_ref, xn_ref):
    @pl.when(pl.program_id(1) == 0)
    def _():
        xn_ref[...] = _rms(x_ref[...], g_ref[...]).astype(BF16)

    o_ref[...] = jnp.dot(xn_ref[...], w_ref[...], preferred_element_type=F32)


def _norm_matmul(x, gain, w, tm, tn):
    n, d = x.shape
    f = w.shape[1]
    return pl.pallas_call(
        _norm_matmul_body,
        out_shape=jax.ShapeDtypeStruct((n, f), F32),
        grid=(n // tm, f // tn),
        in_specs=[pl.BlockSpec((tm, d), lambda i, j: (i, 0)),
                  pl.BlockSpec((1, d), lambda i, j: (0, 0)),
                  pl.BlockSpec((d, tn), lambda i, j: (0, j))],
        out_specs=pl.BlockSpec((tm, tn), lambda i, j: (i, j)),
        scratch_shapes=[pltpu.VMEM((tm, d), BF16)],
        compiler_params=_cparams("parallel", "arbitrary"),
        name="norm_matmul",
    )(x, gain.reshape(1, d), w)


def _matmul_residual_body(n_pairs, res_ref, *refs):
    o_ref = refs[-1]
    acc = res_ref[...]
    for p in range(n_pairs):
        acc = acc + jnp.dot(refs[2 * p][...], refs[2 * p + 1][...],
                            preferred_element_type=F32)
    o_ref[...] = acc


def _matmul_residual(res, pairs, tm):
    n, d = res.shape
    in_specs = [pl.BlockSpec((tm, d), lambda i: (i, 0))]
    args = [res]
    for a, w in pairs:
        in_specs.append(pl.BlockSpec((tm, a.shape[1]), lambda i: (i, 0)))
        in_specs.append(pl.BlockSpec(w.shape, lambda i: (0, 0)))
        args += [a, w]
    return pl.pallas_call(
        functools.partial(_matmul_residual_body, len(pairs)),
        out_shape=jax.ShapeDtypeStruct((n, d), F32),
        grid=(n // tm,),
        in_specs=in_specs,
        out_specs=pl.BlockSpec((tm, d), lambda i: (i, 0)),
        compiler_params=_cparams("parallel"),
        name="matmul_residual",
    )(*args)


def _qkv_prep_body(q_ref, k_ref, v_ref, cos_ref, sin_ref, qg_ref, kg_ref,
                   qo_ref, ko_ref, vt_ref, km_ref):
    lane = _iota((1, LANES), 1)
    first_half = (lane % ATT_HEAD_DIM) < (ATT_HEAD_DIM // 2)
    r = _iota((LANES, LANES), 0) // ATT_HEAD_DIM
    c = _iota((LANES, LANES), 1) // ATT_HEAD_DIM
    head_mean = jnp.where(r == c, 1.0 / ATT_HEAD_DIM, 0.0).astype(BF16)
    cos = cos_ref[...]
    sin = sin_ref[...]

    def prep(x, g):
        sq = x * x
        hi = sq.astype(BF16)
        lo = (sq - hi.astype(F32)).astype(BF16)
        ms = (jnp.dot(hi, head_mean, preferred_element_type=F32)
              + jnp.dot(lo, head_mean, preferred_element_type=F32))
        xn = x * lax.rsqrt(ms + EPS) * g
        half = ATT_HEAD_DIM // 2
        partner = jnp.where(first_half, pltpu.roll(xn, LANES - half, 1), pltpu.roll(xn, half, 1))
        return xn * cos + partner * sin

    qo_ref[0] = prep(q_ref[0], qg_ref[...])
    kk = prep(k_ref[0], kg_ref[...])
    ko_ref[0] = kk.astype(BF16)
    v = v_ref[0]
    for r0 in range(v.shape[0] // MOBA_BLOCK):
        rows = slice(r0 * MOBA_BLOCK, (r0 + 1) * MOBA_BLOCK)
        km_ref[0, r0] = jnp.mean(kk[rows], axis=0, keepdims=True)
        vt_ref[0, r0] = v[rows].T.astype(BF16)


def _qkv_prep(proj, cos, sin, qg, kg):
    b, s, _ = proj.shape
    nb = s // MOBA_BLOCK
    n_pairs = ATT_WIDTH // LANES
    per_step = min(PREP_BLOCKS_PER_STEP, nb)
    rows = per_step * MOBA_BLOCK
    blk = (1, rows, LANES)
    qn, kn, vt, km = pl.pallas_call(
        _qkv_prep_body,
        out_shape=[jax.ShapeDtypeStruct((b, s, ATT_WIDTH), F32),
                   jax.ShapeDtypeStruct((b, s, ATT_WIDTH), BF16),
                   jax.ShapeDtypeStruct((b, nb, ATT_WIDTH, MOBA_BLOCK), BF16),
                   jax.ShapeDtypeStruct((b, nb, 1, ATT_WIDTH), F32)],
        grid=(b, nb // per_step, n_pairs),
        in_specs=[pl.BlockSpec(blk, lambda bi, i, p: (bi, i, p)),
                  pl.BlockSpec(blk, lambda bi, i, p: (bi, i, n_pairs + p)),
                  pl.BlockSpec(blk, lambda bi, i, p: (bi, i, 2 * n_pairs + p)),
                  pl.BlockSpec((rows, LANES), lambda bi, i, p: (i, 0)),
                  pl.BlockSpec((rows, LANES), lambda bi, i, p: (i, 0)),
                  pl.BlockSpec((1, LANES), lambda bi, i, p: (0, 0)),
                  pl.BlockSpec((1, LANES), lambda bi, i, p: (0, 0))],
        out_specs=[pl.BlockSpec(blk, lambda bi, i, p: (bi, i, p)),
                   pl.BlockSpec(blk, lambda bi, i, p: (bi, i, p)),
                   pl.BlockSpec((1, per_step, LANES, MOBA_BLOCK), lambda bi, i, p: (bi, i, p, 0)),
                   pl.BlockSpec((1, per_step, 1, LANES), lambda bi, i, p: (bi, i, 0, p))],
        compiler_params=_cparams("parallel", "parallel", "parallel"),
        name="moba_qkv_prep",
    )(proj, proj, proj, cos, sin, qg, kg)
    return qn, kn, vt, km.reshape(b, nb, ATT_WIDTH)


def _moba_select_body(q_ref, km_ref, o_ref):
    q = q_ref[0]
    km = km_ref[0]
    s = q.shape[0]
    nb = km.shape[0]
    lane = _iota((1, LANES), 1)
    blk = _iota((nb, 1), 0)
    own = _iota((1, s), 1) // MOBA_BLOCK
    past = blk < own
    nt = (((1,), (1,)), ((), ()))
    for hh in range(2):
        qh = jnp.where((lane // ATT_HEAD_DIM) == hh, q, 0.0)
        gate = lax.dot_general(km, qh, nt, precision=HIGHEST, preferred_element_type=F32)
        gate = jnp.where(past, gate, -jnp.inf)
        cnt = jnp.zeros((nb, s), F32)
        for jp in range(nb):
            gj = gate[jp:jp + 1, :]
            cnt = cnt + jnp.where(jp < blk, jnp.where(gj >= gate, 1.0, 0.0),
                                  jnp.where(gj > gate, 1.0, 0.0))
        keep = jnp.logical_and(past, cnt < MOBA_TOPK)
        sel_bias = jnp.where(keep, 0.0, NEG_BIG)
        for i in range(s // MOBA_BLOCK):
            o_ref[0, 0, i, hh * nb:(hh + 1) * nb, :] = sel_bias[:, i * MOBA_BLOCK:(i + 1) * MOBA_BLOCK]


def _moba_select(qn, kmean):
    b, s, _ = qn.shape
    nb = s // MOBA_BLOCK
    n_pairs = ATT_WIDTH // LANES
    return pl.pallas_call(
        _moba_select_body,
        out_shape=jax.ShapeDtypeStruct((b, n_pairs, nb, 2 * nb, MOBA_BLOCK), F32),
        grid=(b, n_pairs),
        in_specs=[pl.BlockSpec((1, s, LANES), lambda bi, p: (bi, 0, p)),
                  pl.BlockSpec((1, nb, LANES), lambda bi, p: (bi, 0, p))],
        out_specs=pl.BlockSpec((1, 1, nb, 2 * nb, MOBA_BLOCK), lambda bi, p: (bi, p, 0, 0, 0)),
        compiler_params=_cparams("parallel", "parallel"),
        name="moba_select",
    )(qn, kmean)


def _moba_body(q_ref, k_ref, vt_ref, sel_ref, o_ref, bias_ref, s_ref, p_ref):
    i = pl.program_id(2)
    bs = MOBA_BLOCK
    qt = MOBA_QUERY_TILE
    q = q_ref[0]
    nb = vt_ref.shape[1]
    scale = ATT_HEAD_DIM ** -0.5 * LOG2_E
    lane = _iota((1, LANES), 1)
    nt = (((1,), (1,)), ((), ()))

    streams = []
    for hh in range(2):
        qh = jnp.where((lane // ATT_HEAD_DIM) == hh, q, 0.0)
        for jp in range(nb):
            bias_ref[hh, jp] = sel_ref[0, 0, 0, hh * nb + jp:hh * nb + jp + 1, :]
        qb_all = (qh * scale).astype(BF16)
        vrows = slice(hh * ATT_HEAD_DIM, (hh + 1) * ATT_HEAD_DIM)
        for qs in range(0, bs, qt):
            streams.append((hh, qs, qb_all[qs:qs + qt, :], vrows))
    n_streams = len(streams)

    def score_tiles(j):
        k_j = k_ref[0, pl.ds(pl.multiple_of(j * bs, bs), bs), :]
        return tuple(lax.dot_general(k_j, qb, nt, preferred_element_type=F32)
                     for (_, _, qb, _) in streams)

    def value_products(j, probs):
        return [jnp.dot(vt_ref[0, j, vrows, :], probs[si], preferred_element_type=F32)
                for si, (_, _, _, vrows) in enumerate(streams)]

    ms, ls, accs = [], [], []
    for si, s in enumerate(score_tiles(i)):
        qs = streams[si][1]
        causal = _iota((bs, qt), 0) <= qs + _iota((bs, qt), 1)
        s = jnp.where(causal, s, NEG_BIG)
        m = jnp.max(s, axis=0, keepdims=True)
        p = jnp.exp2(s - m)
        ms.append(m)
        ls.append(jnp.sum(p, axis=0, keepdims=True))
        accs.append(jnp.zeros((ATT_HEAD_DIM, qt), F32))
        p_ref[1, si] = p.astype(BF16)
    for si, s in enumerate(score_tiles(0)):
        s_ref[0, si] = s

    def handle(j, slot, state):
        ms, ls, accs = state
        for si, s in enumerate(score_tiles(jnp.minimum(j + 1, nb - 1))):
            s_ref[1 - slot, si] = s
        owed = value_products(jnp.where(j == 0, i, j - 1),
                              [p_ref[1 - slot, si] for si in range(n_streams)])
        ms_o, ls_o, accs_o = [], [], []
        for si, (hh, qs, _, _) in enumerate(streams):
            s = s_ref[slot, si] + bias_ref[hh, j, :, qs:qs + qt]
            m_new = jnp.maximum(ms[si], jnp.max(s, axis=0, keepdims=True))
            alpha = jnp.exp2(ms[si] - m_new)
            p = jnp.exp2(s - m_new)
            ms_o.append(m_new)
            ls_o.append(alpha * ls[si] + jnp.sum(p, axis=0, keepdims=True))
            accs_o.append(alpha * (accs[si] + owed[si]))
            p_ref[slot, si] = p.astype(BF16)
        return tuple(ms_o), tuple(ls_o), tuple(accs_o)

    def two_blocks(t, state):
        return handle(2 * t + 1, 1, handle(2 * t, 0, state))

    trips = (i + 1) // 2
    ms, ls, accs = lax.fori_loop(0, trips, two_blocks, (tuple(ms), tuple(ls), tuple(accs)))
    owed = value_products(jnp.where(trips == 0, i, 2 * trips - 1),
                          [p_ref[1, si] for si in range(n_streams)])
    outs = [(accs[si] + owed[si]) / ls[si] for si in range(n_streams)]
    per_head = bs // qt
    out_rows = [jnp.concatenate(outs[hh * per_head:(hh + 1) * per_head], axis=1) for hh in range(2)]
    o_ref[0] = jnp.concatenate(out_rows, axis=0).T.astype(o_ref.dtype)


def _moba_attention(qn, kn, vt, sel):
    b, s, _ = qn.shape
    nb = s // MOBA_BLOCK
    n_pairs = ATT_WIDTH // LANES
    n_streams = 2 * MOBA_BLOCK // MOBA_QUERY_TILE
    return pl.pallas_call(
        _moba_body,
        out_shape=jax.ShapeDtypeStruct((b, s, ATT_WIDTH), BF16),
        grid=(b, n_pairs, nb),
        in_specs=[pl.BlockSpec((1, MOBA_BLOCK, LANES), lambda bi, p, i: (bi, i, p)),
                  pl.BlockSpec((1, s, LANES), lambda bi, p, i: (bi, 0, p)),
                  pl.BlockSpec((1, nb, LANES, MOBA_BLOCK), lambda bi, p, i: (bi, 0, p, 0)),
                  pl.BlockSpec((1, 1, 1, 2 * nb, MOBA_BLOCK), lambda bi, p, i: (bi, p, i, 0, 0))],
        out_specs=pl.BlockSpec((1, MOBA_BLOCK, LANES), lambda bi, p, i: (bi, i, p)),
        scratch_shapes=[pltpu.VMEM((2, nb, 1, MOBA_BLOCK), F32),
                        pltpu.VMEM((2, n_streams, MOBA_BLOCK, MOBA_QUERY_TILE), F32),
                        pltpu.VMEM((2, n_streams, MOBA_BLOCK, MOBA_QUERY_TILE), BF16)],
        compiler_params=_cparams("parallel", "parallel", "arbitrary"),
        name="moba_attention",
    )(qn, kn, vt, sel)


def _pool_body(ts, u_ref, halo_ref, w_ref, sc_ref, o_ref, ext_ref):
    i = pl.program_id(1)
    u = u_ref[0]
    ext_ref[0:POOL_HALO, :] = jnp.where(i > 0, halo_ref[0], 0.0)
    ext_ref[POOL_HALO:POOL_HALO + ts, :] = u
    t = i * ts + _iota((ts, 1), 0)
    outs = []
    for g, win in enumerate(POOL_WINDOWS):
        cols = slice(g * POOL_GROUP_DIM, (g + 1) * POOL_GROUP_DIM)
        ug = u[:, cols]
        acc = ug
        for k in range(1, win):
            acc = acc + ext_ref[POOL_HALO - k:POOL_HALO - k + ts, cols]
        cnt = jnp.minimum(t + 1, win).astype(F32)
        pooled = acc / cnt - ug
        outs.append(jnp.dot(pooled.astype(BF16), w_ref[g], preferred_element_type=F32))
    o_ref[0] = (jnp.concatenate(outs, axis=1) * sc_ref[...]).astype(o_ref.dtype)


def _multiscale_pool(proj, w_pool, pool_scale, ts):
    b, s, f = proj.shape
    width = len(POOL_WINDOWS) * POOL_GROUP_DIM
    assert max(POOL_WINDOWS) <= POOL_HALO and f % width == 0
    col = f // width - 1
    halo_per_tile = ts // POOL_HALO
    return pl.pallas_call(
        functools.partial(_pool_body, ts),
        out_shape=jax.ShapeDtypeStruct((b, s, width), BF16),
        grid=(b, s // ts),
        in_specs=[pl.BlockSpec((1, ts, width), lambda bi, i: (bi, i, col)),
                  pl.BlockSpec((1, POOL_HALO, width),
                               lambda bi, i: (bi, jnp.maximum(i * halo_per_tile - 1, 0), col)),
                  pl.BlockSpec(w_pool.shape, lambda bi, i: (0, 0, 0)),
                  pl.BlockSpec((1, width), lambda bi, i: (0, 0))],
        out_specs=pl.BlockSpec((1, ts, width), lambda bi, i: (bi, i, 0)),
        scratch_shapes=[pltpu.VMEM((POOL_HALO + ts, width), F32)],
        compiler_params=_cparams("parallel", "parallel"),
        name="multiscale_pool",
    )(proj, proj, w_pool, pool_scale.reshape(1, width))


def _xattn_body(h_ref, kv_ref, g_ref, wq_ref, qg_ref, kg_ref, wo_ref, o_ref):
    h = h_ref[0]
    d = h.shape[-1]
    kv = kv_ref[0]
    hn = _rms(h, g_ref[...]).astype(BF16)
    q = jnp.dot(hn, wq_ref[...], preferred_element_type=F32)
    scale = XA_HEAD_DIM ** -0.5
    nt = (((1,), (1,)), ((), ()))
    outs = []
    for hh in range(XA_HEADS):
        cols = slice(hh * XA_HEAD_DIM, (hh + 1) * XA_HEAD_DIM)
        qh = _rms(q[:, cols], qg_ref[...]).astype(BF16)
        kh = _rms(kv[:, cols], kg_ref[...]).astype(BF16)
        vh = kv[:, d + hh * XA_HEAD_DIM:d + (hh + 1) * XA_HEAD_DIM].astype(BF16)
        s = lax.dot_general(qh, kh, nt, preferred_element_type=F32) * scale
        m = jnp.max(s, axis=-1, keepdims=True)
        p = jnp.exp(s - m)
        p = p / jnp.sum(p, axis=-1, keepdims=True)
        outs.append(jnp.dot(p.astype(BF16), vh, preferred_element_type=F32))
    o = jnp.concatenate(outs, axis=1).astype(BF16)
    o_ref[0] = h + jnp.dot(o, wo_ref[...], preferred_element_type=F32)


def _cross_attention(h, kv, gain, wq, q_gain, k_gain, wo, ts):
    b, s, d = h.shape
    m = kv.shape[1]
    return pl.pallas_call(
        _xattn_body,
        out_shape=jax.ShapeDtypeStruct((b, s, d), F32),
        grid=(b, s // ts),
        in_specs=[pl.BlockSpec((1, ts, d), lambda bi, i: (bi, i, 0)),
                  pl.BlockSpec((1, m, 2 * d), lambda bi, i: (bi, 0, 0)),
                  pl.BlockSpec((1, d), lambda bi, i: (0, 0)),
                  pl.BlockSpec((d, d), lambda bi, i: (0, 0)),
                  pl.BlockSpec((1, XA_HEAD_DIM), lambda bi, i: (0, 0)),
                  pl.BlockSpec((1, XA_HEAD_DIM), lambda bi, i: (0, 0)),
                  pl.BlockSpec((d, d), lambda bi, i: (0, 0))],
        out_specs=pl.BlockSpec((1, ts, d), lambda bi, i: (bi, i, 0)),
        compiler_params=_cparams("parallel", "parallel"),
        name="memory_cross_attention",
    )(h, kv, gain.reshape(1, d), wq, q_gain.reshape(1, -1), k_gain.reshape(1, -1), wo)


def _swiglu_body(h_ref, g_ref, wg_ref, wu_ref, wd_ref, o_ref):
    h = h_ref[...]
    hn = _rms(h, g_ref[...]).astype(BF16)
    f = wg_ref.shape[1]
    acc = h
    for c0 in range(0, f, FF_CHUNK):
        cols = slice(c0, min(c0 + FF_CHUNK, f))
        gate = jnp.dot(hn, wg_ref[:, cols], preferred_element_type=F32)
        up = jnp.dot(hn, wu_ref[:, cols], preferred_element_type=F32)
        act = (_silu(gate) * up).astype(BF16)
        acc = acc + jnp.dot(act, wd_ref[cols, :], preferred_element_type=F32)
    o_ref[...] = acc


def _swiglu(h, gain, wg, wu, wd, tm):
    n, d = h.shape
    f = wg.shape[1]
    return pl.pallas_call(
        _swiglu_body,
        out_shape=jax.ShapeDtypeStruct((n, d), F32),
        grid=(n // tm,),
        in_specs=[pl.BlockSpec((tm, d), lambda i: (i, 0)),
                  pl.BlockSpec((1, d), lambda i: (0, 0)),
                  pl.BlockSpec((d, f), lambda i: (0, 0)),
                  pl.BlockSpec((d, f), lambda i: (0, 0)),
                  pl.BlockSpec((f, d), lambda i: (0, 0))],
        out_specs=pl.BlockSpec((tm, d), lambda i: (i, 0)),
        compiler_params=_cparams("parallel"),
        name="swiglu",
    )(h, gain.reshape(1, d), wg, wu, wd)


def _ssd_body(zx_ref, cw_ref, cb_ref, dtb_ref, alog_ref, dskip_ref, ng_ref, o_ref,
              prev_ref, state_ref):
    c = pl.program_id(1)
    L = SSD_CHUNK
    n_state = SSD_STATE
    pair_w = 2 * SSD_HEAD_DIM
    assert pair_w == LANES and n_state == LANES and L == LANES

    @pl.when(c == 0)
    def _():
        prev_ref[...] = jnp.zeros_like(prev_ref)
        state_ref[...] = jnp.zeros_like(state_ref)

    blk = zx_ref[0]
    z = blk[:, :SSD_D_INNER]
    xr = blk[:, SSD_D_INNER:SSD_D_INNER + SSD_CONV_DIM]
    dtr = blk[:, SSD_D_INNER + SSD_CONV_DIM:SSD_D_INNER + SSD_CONV_DIM + LANES]

    tail = prev_ref[...]
    head = xr[0:SUBLANES]
    row = _iota((SUBLANES, 1), 0)
    conv = xr * cw_ref[SSD_CONV - 1:SSD_CONV, :]
    conv_head = head * cw_ref[SSD_CONV - 1:SSD_CONV, :]
    for k in range(1, SSD_CONV):
        tap = cw_ref[SSD_CONV - 1 - k:SSD_CONV - k, :]
        conv = conv + pltpu.roll(xr, k, 0) * tap
        conv_head = conv_head + jnp.where(row < k, pltpu.roll(tail, k, 0),
                                          pltpu.roll(head, k, 0)) * tap
    conv = jnp.concatenate([conv_head, conv[SUBLANES:]], axis=0)
    prev_ref[...] = xr[L - SUBLANES:L]
    xa = _silu(conv + cb_ref[...])
    xs = xa[:, :SSD_D_INNER]
    bm = xa[:, SSD_D_INNER:SSD_D_INNER + SSD_GROUPS * n_state]
    cm = xa[:, SSD_D_INNER + SSD_GROUPS * n_state:]

    dtx = dtr + dtb_ref[...]
    dt = jnp.maximum(dtx, 0.0) + jnp.log(1.0 + jnp.exp(-jnp.abs(dtx)))
    da = dt * (-jnp.exp(alog_ref[...]))
    tri = jnp.where(_iota((L, L), 1) <= _iota((L, L), 0), 1.0, 0.0)
    acum = jnp.dot(tri, da, precision=HIGHEST, preferred_element_type=F32)
    acum_t = acum.T
    dt_t = dt.T
    shifted_t = acum_t - jnp.log(dt_t)
    to_end_t = jnp.exp(acum_t[:, L - 1:L] - acum_t)
    causal = _iota((L, L), 1) <= _iota((L, L), 0)
    lane = _iota((1, LANES), 1)
    low = lane < SSD_HEAD_DIM
    nt = (((1,), (1,)), ((), ()))

    heads_per_group = SSD_HEADS // SSD_GROUPS
    y_pairs = []
    for g in range(SSD_GROUPS):
        bg = bm[:, g * n_state:(g + 1) * n_state]
        cg = cm[:, g * n_state:(g + 1) * n_state].astype(BF16)
        bg_t = bg.T
        cb = lax.dot_general(cg, bg.astype(BF16), nt, preferred_element_type=F32)
        for pp in range(heads_per_group // 2):
            pidx = g * (heads_per_group // 2) + pp
            cols = slice(pidx * pair_w, (pidx + 1) * pair_w)
            x_pair = xs[:, cols].astype(BF16)
            st_pair = state_ref[:, cols]
            cs = jnp.dot(cg, st_pair.astype(BF16), preferred_element_type=F32)
            ys, upds, lasts = [], [], []
            for hh in range(2):
                h = 2 * pidx + hh
                bc = jnp.broadcast_to(acum[:, h:h + 1], (L, L))
                seg = bc - shifted_t[h:h + 1, :]
                dec = jnp.exp(jnp.where(causal, seg, -jnp.inf))
                mm = (cb * dec).astype(BF16)
                ebc = jnp.exp(bc)
                ys.append(jnp.dot(mm, x_pair, preferred_element_type=F32) + ebc * cs)
                wrow = dt_t[h:h + 1, :] * to_end_t[h:h + 1, :]
                upds.append(jnp.dot((bg_t * wrow).astype(BF16), x_pair,
                                    preferred_element_type=F32))
                lasts.append(ebc[L - 1:L, :])
            y_pairs.append(jnp.where(low, ys[0], ys[1]))
            state_ref[:, cols] = (st_pair * jnp.where(low, lasts[0], lasts[1])
                                  + jnp.where(low, upds[0], upds[1]))

    y = jnp.concatenate(y_pairs, axis=1)
    y = (y + dskip_ref[...] * xs) * _silu(z)
    gw = SSD_D_INNER // SSD_GROUPS
    outs = []
    for g in range(SSD_GROUPS):
        cols = slice(g * gw, (g + 1) * gw)
        outs.append(_rms(y[:, cols], ng_ref[:, cols]))
    o_ref[0] = jnp.concatenate(outs, axis=1).astype(o_ref.dtype)


def _ssd_mixer(zx, conv_w, conv_b, dt_bias, a_log, d_skip, norm_g):
    b, s, f = zx.shape
    pad = LANES - SSD_HEADS
    dtb = jnp.pad(dt_bias, (0, pad)).reshape(1, LANES)
    alog = jnp.pad(a_log, (0, pad)).reshape(1, LANES)
    dskip = jnp.repeat(d_skip, SSD_HEAD_DIM).reshape(1, SSD_D_INNER)
    full = lambda shape: pl.BlockSpec(shape, lambda bi, c: (0,) * len(shape))
    return pl.pallas_call(
        _ssd_body,
        out_shape=jax.ShapeDtypeStruct((b, s, SSD_D_INNER), BF16),
        grid=(b, s // SSD_CHUNK),
        in_specs=[pl.BlockSpec((1, SSD_CHUNK, f), lambda bi, c: (bi, c, 0)),
                  full((SSD_CONV, SSD_CONV_DIM)), full((1, SSD_CONV_DIM)),
                  full((1, LANES)), full((1, LANES)),
                  full((1, SSD_D_INNER)), full((1, SSD_D_INNER))],
        out_specs=pl.BlockSpec((1, SSD_CHUNK, SSD_D_INNER), lambda bi, c: (bi, c, 0)),
        scratch_shapes=[pltpu.VMEM((SUBLANES, SSD_CONV_DIM), F32),
                        pltpu.VMEM((SSD_STATE, SSD_D_INNER), F32)],
        compiler_params=_cparams("parallel", "arbitrary"),
        name="ssd_mixer",
    )(zx, conv_w, conv_b.reshape(1, -1), dtb, alog, dskip, norm_g.reshape(1, -1))


def _router_body(h_ref, g_ref, wr_ref, hn_ref, lg_ref):
    hn = _rms(h_ref[...], g_ref[...])
    _matrix_to_rows(hn_ref, hn)
    lg_ref[...] = jnp.dot(hn, wr_ref[...], precision=HIGHEST, preferred_element_type=F32)


def _router(h, gain, w_router_pad, tm):
    n, d = h.shape
    assert d == SUBLANES * LANES
    hn_rows, logits = pl.pallas_call(
        _router_body,
        out_shape=[jax.ShapeDtypeStruct((n * SUBLANES, LANES), F32),
                   jax.ShapeDtypeStruct((n, LANES), F32)],
        grid=(n // tm,),
        in_specs=[pl.BlockSpec((tm, d), lambda i: (i, 0)),
                  pl.BlockSpec((1, d), lambda i: (0, 0)),
                  pl.BlockSpec((d, LANES), lambda i: (0, 0))],
        out_specs=[pl.BlockSpec((tm * SUBLANES, LANES), lambda i: (i, 0)),
                   pl.BlockSpec((tm, LANES), lambda i: (i, 0))],
        compiler_params=_cparams("parallel"),
        name="moe_router",
    )(h, gain.reshape(1, d), w_router_pad)
    return hn_rows.reshape(n, SUBLANES, LANES), logits


def _matrix_to_rows(rows_ref, x):
    n = x.shape[0]
    for j in range(x.shape[1] // LANES):
        rows_ref[pl.ds(j, n, stride=SUBLANES), :] = x[:, j * LANES:(j + 1) * LANES]


def _row_copy(src_hbm, dst_ref, sems, slot, idx, r):
    rows = pl.ds(pl.multiple_of(r * SUBLANES, SUBLANES), SUBLANES)
    return pltpu.make_async_copy(src_hbm.at[idx], dst_ref.at[slot, rows, :], sems.at[slot])


def _start_row_gather(idx_ref, src_hbm, dst_ref, sems, slot, n_rows):
    def body(r, carry):
        _row_copy(src_hbm, dst_ref, sems, slot, idx_ref[0, 0, r], r).start()
        return carry

    lax.fori_loop(0, n_rows, body, 0, unroll=8)


def _wait_row_gather(src_hbm, dst_ref, sems, slot, n_rows):
    def body(r, carry):
        _row_copy(src_hbm, dst_ref, sems, slot, 0, r).wait()
        return carry

    lax.fori_loop(0, n_rows, body, 0, unroll=8)


def _rows_to_matrix(buf_ref, slot, first_row, n_rows, d):
    chunks = [buf_ref[slot, pl.ds(first_row * SUBLANES + j, n_rows, stride=SUBLANES), :]
              for j in range(d // LANES)]
    return jnp.concatenate(chunks, axis=1)


def _expert_body(blk_e_ref, n_valid_ref, tok_ref, tok_next_ref, hn_hbm, gate_ref,
                 wg_ref, wu_ref, wd_ref, o_ref, x_ref, sems):
    i = pl.program_id(0)
    last = pl.num_programs(0) - 1
    rows = gate_ref.shape[0]
    d = wg_ref.shape[1]
    slot = i % 2

    @pl.when(jnp.logical_and(i == 0, n_valid_ref[0] > 0))
    def _():
        _start_row_gather(tok_ref, hn_hbm, x_ref, sems, 0, rows)

    @pl.when(jnp.logical_and(i < last, n_valid_ref[jnp.minimum(i + 1, last)] > 0))
    def _():
        _start_row_gather(tok_next_ref, hn_hbm, x_ref, sems, 1 - slot, rows)

    @pl.when(n_valid_ref[i] > 0)
    def _():
        _wait_row_gather(hn_hbm, x_ref, sems, slot, rows)
        x = _rows_to_matrix(x_ref, slot, 0, rows, d).astype(BF16)
        f = wg_ref.shape[2]
        acc = jnp.zeros((rows, d), F32)
        for c0 in range(0, f, FF_CHUNK):
            cols = slice(c0, min(c0 + FF_CHUNK, f))
            gate = jnp.dot(x, wg_ref[0, :, cols], preferred_element_type=F32)
            up = jnp.dot(x, wu_ref[0, :, cols], preferred_element_type=F32)
            act = (_silu(gate) * up).astype(BF16)
            acc = acc + jnp.dot(act, wd_ref[0, cols, :], preferred_element_type=F32)
        _matrix_to_rows(o_ref, acc * gate_ref[...])

    @pl.when(n_valid_ref[i] == 0)
    def _():
        o_ref[...] = jnp.zeros_like(o_ref)


def _expert_ffn(blk_e, n_valid, row_tok, hn_rows, row_gate, wg, wu, wd):
    n_blocks = blk_e.shape[0]
    d = wg.shape[1]
    f = wg.shape[2]
    rows = MOE_ROWS
    tok = row_tok.reshape(n_blocks, 1, rows)
    grid_spec = pltpu.PrefetchScalarGridSpec(
        num_scalar_prefetch=2,
        grid=(n_blocks,),
        in_specs=[pl.BlockSpec((1, 1, rows), lambda i, be, nv: (i, 0, 0), memory_space=pltpu.SMEM),
                  pl.BlockSpec((1, 1, rows), lambda i, be, nv: (jnp.minimum(i + 1, n_blocks - 1), 0, 0),
                               memory_space=pltpu.SMEM),
                  pl.BlockSpec(memory_space=pl.ANY),
                  pl.BlockSpec((rows, 1), lambda i, be, nv: (i, 0)),
                  pl.BlockSpec((1, d, f), lambda i, be, nv: (be[i], 0, 0)),
                  pl.BlockSpec((1, d, f), lambda i, be, nv: (be[i], 0, 0)),
                  pl.BlockSpec((1, f, d), lambda i, be, nv: (be[i], 0, 0))],
        out_specs=pl.BlockSpec((rows * SUBLANES, LANES), lambda i, be, nv: (i, 0)),
        scratch_shapes=[pltpu.VMEM((2, rows * SUBLANES, LANES), F32),
                        pltpu.SemaphoreType.DMA((2,))],
    )
    y_rows = pl.pallas_call(
        _expert_body,
        out_shape=jax.ShapeDtypeStruct((n_blocks * rows * SUBLANES, LANES), F32),
        grid_spec=grid_spec,
        compiler_params=_cparams("arbitrary"),
        name="moe_expert_ffn",
    )(blk_e, n_valid, tok, tok, hn_rows, row_gate.reshape(-1, 1), wg, wu, wd)
    return y_rows.reshape(n_blocks * rows, SUBLANES, LANES)


def _combine_body(pos_ref, pos_next_ref, h_ref, y_hbm, o_ref, y_ref, sems):
    i = pl.program_id(0)
    last = pl.num_programs(0) - 1
    tc, d = o_ref.shape
    slot = i % 2

    @pl.when(i == 0)
    def _():
        _start_row_gather(pos_ref, y_hbm, y_ref, sems, 0, TOP_K * tc)

    @pl.when(i < last)
    def _():
        _start_row_gather(pos_next_ref, y_hbm, y_ref, sems, 1 - slot, TOP_K * tc)

    _wait_row_gather(y_hbm, y_ref, sems, slot, TOP_K * tc)
    acc = h_ref[...]
    for k in range(TOP_K):
        acc = acc + _rows_to_matrix(y_ref, slot, k * tc, tc, d)
    o_ref[...] = acc


def _moe_combine(h, pos, y_rows):
    n, d = h.shape
    tc = COMBINE_TOKENS
    n_tiles = n // tc
    pos_tiles = pos.reshape(n_tiles, tc, TOP_K).transpose(0, 2, 1).reshape(n_tiles, 1, TOP_K * tc)
    return pl.pallas_call(
        _combine_body,
        out_shape=jax.ShapeDtypeStruct((n, d), F32),
        grid=(n_tiles,),
        in_specs=[pl.BlockSpec((1, 1, TOP_K * tc), lambda i: (i, 0, 0), memory_space=pltpu.SMEM),
                  pl.BlockSpec((1, 1, TOP_K * tc), lambda i: (jnp.minimum(i + 1, n_tiles - 1), 0, 0),
                               memory_space=pltpu.SMEM),
                  pl.BlockSpec((tc, d), lambda i: (i, 0)),
                  pl.BlockSpec(memory_space=pl.ANY)],
        out_specs=pl.BlockSpec((tc, d), lambda i: (i, 0)),
        scratch_shapes=[pltpu.VMEM((2, TOP_K * tc * SUBLANES, LANES), F32),
                        pltpu.SemaphoreType.DMA((2,))],
        compiler_params=_cparams("arbitrary"),
        name="moe_combine",
    )(pos_tiles, pos_tiles, h, y_rows)


def _moe_layer(h, gain, w_router, wg, wu, wd, tm):
    n, d = h.shape
    e = w_router.shape[1]
    rows = MOE_ROWS
    n_slots = n * TOP_K
    hn_rows, logits = _router(h, gain, jnp.pad(w_router, ((0, 0), (0, LANES - e))), tm)
    top_logit, top_e = lax.top_k(logits[:, :e], TOP_K)
    gates = jax.nn.softmax(top_logit, axis=-1)

    e_flat = top_e.reshape(-1).astype(jnp.int32)
    slot_ids = jnp.arange(n_slots, dtype=jnp.int32)
    e_sorted, order = lax.sort((e_flat, slot_ids), num_keys=1, is_stable=True)
    experts = jnp.arange(e, dtype=jnp.int32)
    sorted_onehot = e_sorted[:, None] == experts[None, :]
    counts = jnp.sum(sorted_onehot, axis=0, dtype=jnp.int32)
    start = jnp.cumsum(counts) - counts
    padded = (counts + rows - 1) // rows * rows
    pad_end = jnp.cumsum(padded)
    pad_start = pad_end - padded
    shift = jnp.sum(jnp.where(sorted_onehot, (pad_start - start)[None, :], 0), axis=1)
    dest = slot_ids + shift
    _, pos = lax.sort((order, dest), num_keys=1)
    n_blocks = n_slots // rows + e
    n_rows = n_blocks * rows
    blk_start = jnp.arange(n_blocks, dtype=jnp.int32) * rows
    blk_e = jnp.minimum(jnp.sum(blk_start[:, None] >= pad_end[None, :], axis=1),
                        e - 1).astype(jnp.int32)
    n_valid = jnp.clip(pad_start[blk_e] + counts[blk_e] - blk_start, 0, rows).astype(jnp.int32)
    row_rank = (jnp.arange(rows, dtype=jnp.int32)[None, :] + (blk_start - pad_start[blk_e])[:, None])
    row_ok = row_rank < counts[blk_e][:, None]
    src = jnp.where(row_ok, start[blk_e][:, None] + row_rank, 0).reshape(-1)
    row_slot = order[src]
    row_tok = jnp.where(row_ok.reshape(-1), row_slot // TOP_K, 0)
    row_gate = jnp.where(row_ok.reshape(-1), gates.reshape(-1)[row_slot], 0.0)

    y_rows = _expert_ffn(blk_e, n_valid, row_tok, hn_rows, row_gate, wg, wu, wd)
    return _moe_combine(h, pos, y_rows)


def _rope_tables(positions):
    half = ATT_HEAD_DIM // 2
    inv_freq = ROPE_THETA ** (-jnp.arange(half, dtype=F32) / half)
    ang = positions.astype(F32)[:, None] * inv_freq[None, :]
    cos = jnp.cos(ang)
    sin = jnp.sin(ang)
    reps = LANES // ATT_HEAD_DIM
    return (jnp.tile(cos, (1, 2 * reps)), jnp.tile(jnp.concatenate([-sin, sin], axis=1), (1, reps)))


def kernel(x, mem, positions, mix_norm, xa_norm, mem_norm, ffn_norm, xa_wq, xa_wkv, xa_q_norm, xa_k_norm, xa_wo, hy_w_in, hy_q_norm, hy_k_norm, pool_w, pool_scale, hy_w_out, ffn_w_gate, ffn_w_up, ffn_w_down, ssd_w_in, ssd_conv_w, ssd_conv_b, ssd_dt_bias, ssd_a_log, ssd_d, ssd_norm, ssd_w_out, moe_router, moe_w_gate, moe_w_up, moe_w_down):
    b, s, d = x.shape
    n = b * s
    n_mem = mem.shape[1]
    depth = mix_norm.shape[0]
    tm = min(512, n)
    ts = min(512, s)
    bf = lambda w: w.astype(BF16)
    cos, sin = _rope_tables(positions)
    reps = LANES // ATT_HEAD_DIM

    h = x.reshape(n, d)
    mem2 = mem.reshape(b * n_mem, d)
    for layer in range(depth):
        j = layer // 2
        if layer % 2 == 0:
            proj = _norm_matmul(h, mix_norm[layer], bf(hy_w_in[j]), min(1024, n), 2048).reshape(b, s, -1)
            qn, kn, vt, kmean = _qkv_prep(proj, cos, sin,
                                          jnp.tile(hy_q_norm[j], reps).reshape(1, LANES),
                                          jnp.tile(hy_k_norm[j], reps).reshape(1, LANES))
            att = _moba_attention(qn, kn, vt, _moba_select(qn, kmean))
            pooled = _multiscale_pool(proj, bf(pool_w[j]), pool_scale[j], ts)
            w_out = bf(hy_w_out[j])
            h = _matmul_residual(h, [(att.reshape(n, -1), w_out[:ATT_WIDTH]),
                                     (pooled.reshape(n, -1), w_out[ATT_WIDTH:])], tm)
        else:
            w_in = ssd_w_in[j]
            w_cat = jnp.pad(w_in, ((0, 0), (0, SSD_DT_PAD - SSD_HEADS)))
            zx = _norm_matmul(h, mix_norm[layer], bf(w_cat), min(1024, n), SSD_PROJ_TILE).reshape(b, s, -1)
            y = _ssd_mixer(zx, ssd_conv_w[j], ssd_conv_b[j], ssd_dt_bias[j], ssd_a_log[j],
                           ssd_d[j], ssd_norm[j])
            h = _matmul_residual(h, [(y.reshape(n, -1), bf(ssd_w_out[j]))], tm)

        kv = _norm_matmul(mem2, mem_norm[layer], bf(xa_wkv[layer]), min(512, b * n_mem), 1024)
        h = _cross_attention(h.reshape(b, s, d), kv.reshape(b, n_mem, 2 * d), xa_norm[layer],
                             bf(xa_wq[layer]), xa_q_norm[layer], xa_k_norm[layer],
                             bf(xa_wo[layer]), ts).reshape(n, d)

        if layer % 2 == 0:
            h = _swiglu(h, ffn_norm[layer], bf(ffn_w_gate[j]), bf(ffn_w_up[j]),
                        bf(ffn_w_down[j]), tm)
        else:
            h = _moe_layer(h, ffn_norm[layer], moe_router[j], bf(moe_w_gate[j]),
                           bf(moe_w_up[j]), bf(moe_w_down[j]), tm)
    return h.reshape(b, s, d)
```

```python
import functools
import math

import jax
import jax.numpy as jnp
from jax import lax
from jax.experimental import pallas as pl
from jax.experimental.pallas import tpu as pltpu

F32 = jnp.float32
BF16 = jnp.bfloat16
HIGHEST = lax.Precision.HIGHEST

EPS = 1e-6
ROPE_THETA = 10000.0
LANES = 128
SUBLANES = 8
VMEM_LIMIT_BYTES = 56 * 1024 * 1024

ATT_HEADS = 8
ATT_HEAD_DIM = 64
ATT_WIDTH = ATT_HEADS * ATT_HEAD_DIM
POOL_WINDOWS = (2, 4, 8, 16)
POOL_GROUP_DIM = 128
POOL_HALO = 16
MOBA_BLOCK = 256
MOBA_TOPK = 3
MOBA_QUERY_TILE = 128
PREP_BLOCKS_PER_STEP = 4
LOG2_E = 1.4426950408889634
NEG_BIG = -1e30

SSD_D_INNER = 2048
SSD_HEAD_DIM = 64
SSD_HEADS = 32
SSD_GROUPS = 4
SSD_STATE = 128
SSD_CONV = 4
SSD_CHUNK = 128
SSD_CONV_DIM = SSD_D_INNER + 2 * SSD_GROUPS * SSD_STATE
SSD_DT_PAD = 256
SSD_PROJ_TILE = 1792

XA_HEADS = 4
XA_HEAD_DIM = 256

N_EXPERTS = 8
TOP_K = 2
MOE_ROWS = 512
FF_CHUNK = 512
COMBINE_TOKENS = 256


def _cparams(*sem):
    return pltpu.CompilerParams(dimension_semantics=sem, vmem_limit_bytes=VMEM_LIMIT_BYTES)


def _rms(x, g):
    return x * lax.rsqrt(jnp.mean(x * x, axis=-1, keepdims=True) + EPS) * g


def _silu(x):
    return x / (1.0 + jnp.exp(-x))


def _iota(shape, dim):
    return lax.broadcasted_iota(jnp.int32, shape, dim)


def _norm_matmul_body(x_ref, g_ref, w_ref, o_ref, xn_ref):
    @pl.when(pl.program_id(1) == 0)
    def _():
        xn_ref[...] = _rms(x_ref[...], g_ref[...]).astype(BF16)

    o_ref[...] = jnp.dot(xn_ref[...], w_ref[...], preferred_element_type=F32)


def _norm_matmul(x, gain, w, tm, tn):
    n, d = x.shape
    f = w.shape[1]
    return pl.pallas_call(
        _norm_matmul_body,
        out_shape=jax.ShapeDtypeStruct((n, f), F32),
        grid=(n // tm, f // tn),
        in_specs=[pl.BlockSpec((tm, d), lambda i, j: (i, 0)),
                  pl.BlockSpec((1, d), lambda i, j: (0, 0)),
                  pl.BlockSpec((d, tn), lambda i, j: (0, j))],
        out_specs=pl.BlockSpec((tm, tn), lambda i, j: (i, j)),
        scratch_shapes=[pltpu.VMEM((tm, d), BF16)],
        compiler_params=_cparams("parallel", "arbitrary"),
        name="norm_matmul",
    )(x, gain.reshape(1, d), w)


def _matmul_residual_body(n_pairs, res_ref, *refs):
    o_ref = refs[-1]
    acc = res_ref[...]
    for p in range(n_pairs):
        acc = acc + jnp.dot(refs[2 * p][...], refs[2 * p + 1][...],
                            preferred_element_type=F32)
    o_ref[...] = acc


def _matmul_residual(res, pairs, tm):
    n, d = res.shape
    in_specs = [pl.BlockSpec((tm, d), lambda i: (i, 0))]
    args = [res]
    for a, w in pairs:
        in_specs.append(pl.BlockSpec((tm, a.shape[1]), lambda i: (i, 0)))
        in_specs.append(pl.BlockSpec(w.shape, lambda i: (0, 0)))
        args += [a, w]
    return pl.pallas_call(
        functools.partial(_matmul_residual_body, len(pairs)),
        out_shape=jax.ShapeDtypeStruct((n, d), F32),
        grid=(n // tm,),
        in_specs=in_specs,
        out_specs=pl.BlockSpec((tm, d), lambda i: (i, 0)),
        compiler_params=_cparams("parallel"),
        name="matmul_residual",
    )(*args)


def _qkv_prep_body(q_ref, k_ref, v_ref, cos_ref, sin_ref, qg_ref, kg_ref,
                   qo_ref, ko_ref, vt_ref, km_ref):
    lane = _iota((1, LANES), 1)
    first_half = (lane % ATT_HEAD_DIM) < (ATT_HEAD_DIM // 2)
    r = _iota((LANES, LANES), 0) // ATT_HEAD_DIM
    c = _iota((LANES, LANES), 1) // ATT_HEAD_DIM
    head_mean = jnp.where(r == c, 1.0 / ATT_HEAD_DIM, 0.0).astype(BF16)
    cos = cos_ref[...]
    sin = sin_ref[...]

    def prep(x, g):
        sq = x * x
        hi = sq.astype(BF16)
        lo = (sq - hi.astype(F32)).astype(BF16)
        ms = (jnp.dot(hi, head_mean, preferred_element_type=F32)
              + jnp.dot(lo, head_mean, preferred_element_type=F32))
        xn = x * lax.rsqrt(ms + EPS) * g
        half = ATT_HEAD_DIM // 2
        partner = jnp.where(first_half, pltpu.roll(xn, LANES - half, 1), pltpu.roll(xn, half, 1))
        return xn * cos + partner * sin

    qo_ref[0] = prep(q_ref[0], qg_ref[...])
    kk = prep(k_ref[0], kg_ref[...])
    ko_ref[0] = kk.astype(BF16)
    v = v_ref[0]
    for r0 in range(v.shape[0] // MOBA_BLOCK):
        rows = slice(r0 * MOBA_BLOCK, (r0 + 1) * MOBA_BLOCK)
        km_ref[0, r0] = jnp.mean(kk[rows], axis=0, keepdims=True)
        vt_ref[0, r0] = v[rows].T.astype(BF16)


def _qkv_prep(proj, cos, sin, qg, kg):
    b, s, _ = proj.shape
    nb = s // MOBA_BLOCK
    n_pairs = ATT_WIDTH // LANES
    per_step = min(PREP_BLOCKS_PER_STEP, nb)
    rows = per_step * MOBA_BLOCK
    blk = (1, rows, LANES)
    qn, kn, vt, km = pl.pallas_call(
        _qkv_prep_body,
        out_shape=[jax.ShapeDtypeStruct((b, s, ATT_WIDTH), F32),
                   jax.ShapeDtypeStruct((b, s, ATT_WIDTH), BF16),
                   jax.ShapeDtypeStruct((b, nb, ATT_WIDTH, MOBA_BLOCK), BF16),
                   jax.ShapeDtypeStruct((b, nb, 1, ATT_WIDTH), F32)],
        grid=(b, nb // per_step, n_pairs),
        in_specs=[pl.BlockSpec(blk, lambda bi, i, p: (bi, i, p)),
                  pl.BlockSpec(blk, lambda bi, i, p: (bi, i, n_pairs + p)),
                  pl.BlockSpec(blk, lambda bi, i, p: (bi, i, 2 * n_pairs + p)),
                  pl.BlockSpec((rows, LANES), lambda bi, i, p: (i, 0)),
                  pl.BlockSpec((rows, LANES), lambda bi, i, p: (i, 0)),
                  pl.BlockSpec((1, LANES), lambda bi, i, p: (0, 0)),
                  pl.BlockSpec((1, LANES), lambda bi, i, p: (0, 0))],
        out_specs=[pl.BlockSpec(blk, lambda bi, i, p: (bi, i, p)),
                   pl.BlockSpec(blk, lambda bi, i, p: (bi, i, p)),
                   pl.BlockSpec((1, per_step, LANES, MOBA_BLOCK), lambda bi, i, p: (bi, i, p, 0)),
                   pl.BlockSpec((1, per_step, 1, LANES), lambda bi, i, p: (bi, i, 0, p))],
        compiler_params=_cparams("parallel", "parallel", "parallel"),
        name="moba_qkv_prep",
    )(proj, proj, proj, cos, sin, qg, kg)
    return qn, kn, vt, km.reshape(b, nb, ATT_WIDTH)


def _moba_select_body(q_ref, km_ref, o_ref):
    q = q_ref[0]
    km = km_ref[0]
    s = q.shape[0]
    nb = km.shape[0]
    lane = _iota((1, LANES), 1)
    blk = _iota((nb, 1), 0)
    own = _iota((1, s), 1) // MOBA_BLOCK
    past = blk < own
    nt = (((1,), (1,)), ((), ()))
    for hh in range(2):
        qh = jnp.where((lane // ATT_HEAD_DIM) == hh, q, 0.0)
        gate = lax.dot_general(km, qh, nt, precision=HIGHEST, preferred_element_type=F32)
        gate = jnp.where(past, gate, -jnp.inf)
        cnt = jnp.zeros((nb, s), F32)
        for jp in range(nb):
            gj = gate[jp:jp + 1, :]
            cnt = cnt + jnp.where(jp < blk, jnp.where(gj >= gate, 1.0, 0.0),
                                  jnp.where(gj > gate, 1.0, 0.0))
        keep = jnp.logical_and(past, cnt < MOBA_TOPK)
        sel_bias = jnp.where(keep, 0.0, NEG_BIG)
        for i in range(s // MOBA_BLOCK):
            o_ref[0, 0, i, hh * nb:(hh + 1) * nb, :] = sel_bias[:, i * MOBA_BLOCK:(i + 1) * MOBA_BLOCK]


def _moba_select(qn, kmean):
    b, s, _ = qn.shape
    nb = s // MOBA_BLOCK
    n_pairs = ATT_WIDTH // LANES
    return pl.pallas_call(
        _moba_select_body,
        out_shape=jax.ShapeDtypeStruct((b, n_pairs, nb, 2 * nb, MOBA_BLOCK), F32),
        grid=(b, n_pairs),
        in_specs=[pl.BlockSpec((1, s, LANES), lambda bi, p: (bi, 0, p)),
                  pl.BlockSpec((1, nb, LANES), lambda bi, p: (bi, 0, p))],
        out_specs=pl.BlockSpec((1, 1, nb, 2 * nb, MOBA_BLOCK), lambda bi, p: (bi, p, 0, 0, 0)),
        compiler_params=_cparams("parallel", "parallel"),
        name="moba_select",
    )(qn, kmean)


def _moba_body(q_ref, k_ref, vt_ref, sel_ref, o_ref, bias_ref, s_ref, p_ref):
    i = pl.program_id(2)
    bs = MOBA_BLOCK
    qt = MOBA_QUERY_TILE
    q = q_ref[0]
    nb = vt_ref.shape[1]
    scale = ATT_HEAD_DIM ** -0.5 * LOG2_E
    lane = _iota((1, LANES), 1)
    nt = (((1,), (1,)), ((), ()))

    streams = []
    for hh in range(2):
        qh = jnp.where((lane // ATT_HEAD_DIM) == hh, q, 0.0)
        for jp in range(nb):
            bias_ref[hh, jp] = sel_ref[0, 0, 0, hh * nb + jp:hh * nb + jp + 1, :]
        qb_all = (qh * scale).astype(BF16)
        vrows = slice(hh * ATT_HEAD_DIM, (hh + 1) * ATT_HEAD_DIM)
        for qs in range(0, bs, qt):
            streams.append((hh, qs, qb_all[qs:qs + qt, :], vrows))
    n_streams = len(streams)

    def score_tiles(j):
        k_j = k_ref[0, pl.ds(pl.multiple_of(j * bs, bs), bs), :]
        return tuple(lax.dot_general(k_j, qb, nt, preferred_element_type=F32)
                     for (_, _, qb, _) in streams)

    def value_products(j, probs):
        return [jnp.dot(vt_ref[0, j, vrows, :], probs[si], preferred_element_type=F32)
                for si, (_, _, _, vrows) in enumerate(streams)]

    ms, ls, accs = [], [], []
    for si, s in enumerate(score_tiles(i)):
        qs = streams[si][1]
        causal = _iota((bs, qt), 0) <= qs + _iota((bs, qt), 1)
        s = jnp.where(causal, s, NEG_BIG)
        m = jnp.max(s, axis=0, keepdims=True)
        p = jnp.exp2(s - m)
        ms.append(m)
        ls.append(jnp.sum(p, axis=0, keepdims=True))
        accs.append(jnp.zeros((ATT_HEAD_DIM, qt), F32))
        p_ref[1, si] = p.astype(BF16)
    for si, s in enumerate(score_tiles(0)):
        s_ref[0, si] = s

    def handle(j, slot, state):
        ms, ls, accs = state
        for si, s in enumerate(score_tiles(jnp.minimum(j + 1, nb - 1))):
            s_ref[1 - slot, si] = s
        owed = value_products(jnp.where(j == 0, i, j - 1),
                              [p_ref[1 - slot, si] for si in range(n_streams)])
        ms_o, ls_o, accs_o = [], [], []
        for si, (hh, qs, _, _) in enumerate(streams):
            s = s_ref[slot, si] + bias_ref[hh, j, :, qs:qs + qt]
            m_new = jnp.maximum(ms[si], jnp.max(s, axis=0, keepdims=True))
            alpha = jnp.exp2(ms[si] - m_new)
            p = jnp.exp2(s - m_new)
            ms_o.append(m_new)
            ls_o.append(alpha * ls[si] + jnp.sum(p, axis=0, keepdims=True))
            accs_o.append(alpha * (accs[si] + owed[si]))
            p_ref[slot, si] = p.astype(BF16)
        return tuple(ms_o), tuple(ls_o), tuple(accs_o)

    def two_blocks(t, state):
        return handle(2 * t + 1, 1, handle(2 * t, 0, state))

    trips = (i + 1) // 2
    ms, ls, accs = lax.fori_loop(0, trips, two_blocks, (tuple(ms), tuple(ls), tuple(accs)))
    owed = value_products(jnp.where(trips == 0, i, 2 * trips - 1),
                          [p_ref[1, si] for si in range(n_streams)])
    outs = [(accs[si] + owed[si]) / ls[si] for si in range(n_streams)]
    per_head = bs // qt
    out_rows = [jnp.concatenate(outs[hh * per_head:(hh + 1) * per_head], axis=1) for hh in range(2)]
    o_ref[0] = jnp.concatenate(out_rows, axis=0).T.astype(o_ref.dtype)


def _moba_attention(qn, kn, vt, sel):
    b, s, _ = qn.shape
    nb = s // MOBA_BLOCK
    n_pairs = ATT_WIDTH // LANES
    n_streams = 2 * MOBA_BLOCK // MOBA_QUERY_TILE
    return pl.pallas_call(
        _moba_body,
        out_shape=jax.ShapeDtypeStruct((b, s, ATT_WIDTH), BF16),
        grid=(b, n_pairs, nb),
        in_specs=[pl.BlockSpec((1, MOBA_BLOCK, LANES), lambda bi, p, i: (bi, i, p)),
                  pl.BlockSpec((1, s, LANES), lambda bi, p, i: (bi, 0, p)),
                  pl.BlockSpec((1, nb, LANES, MOBA_BLOCK), lambda bi, p, i: (bi, 0, p, 0)),
                  pl.BlockSpec((1, 1, 1, 2 * nb, MOBA_BLOCK), lambda bi, p, i: (bi, p, i, 0, 0))],
        out_specs=pl.BlockSpec((1, MOBA_BLOCK, LANES), lambda bi, p, i: (bi, i, p)),
        scratch_shapes=[pltpu.VMEM((2, nb, 1, MOBA_BLOCK), F32),
                        pltpu.VMEM((2, n_streams, MOBA_BLOCK, MOBA_QUERY_TILE), F32),
                        pltpu.VMEM((2, n_streams, MOBA_BLOCK, MOBA_QUERY_TILE), BF16)],
        compiler_params=_cparams("parallel", "parallel", "arbitrary"),
        name="moba_attention",
    )(qn, kn, vt, sel)


def _pool_body(ts, u_ref, halo_ref, w_ref, sc_ref, o_ref, ext_ref):
    i = pl.program_id(1)
    u = u_ref[0]
    ext_ref[0:POOL_HALO, :] = jnp.where(i > 0, halo_ref[0], 0.0)
    ext_ref[POOL_HALO:POOL_HALO + ts, :] = u
    t = i * ts + _iota((ts, 1), 0)
    outs = []
    for g, win in enumerate(POOL_WINDOWS):
        cols = slice(g * POOL_GROUP_DIM, (g + 1) * POOL_GROUP_DIM)
        ug = u[:, cols]
        acc = ug
        for k in range(1, win):
            acc = acc + ext_ref[POOL_HALO - k:POOL_HALO - k + ts, cols]
        cnt = jnp.minimum(t + 1, win).astype(F32)
        pooled = acc / cnt - ug
        outs.append(jnp.dot(pooled.astype(BF16), w_ref[g], preferred_element_type=F32))
    o_ref[0] = (jnp.concatenate(outs, axis=1) * sc_ref[...]).astype(o_ref.dtype)


def _multiscale_pool(proj, w_pool, pool_scale, ts):
    b, s, f = proj.shape
    width = len(POOL_WINDOWS) * POOL_GROUP_DIM
    assert max(POOL_WINDOWS) <= POOL_HALO and f % width == 0
    col = f // width - 1
    halo_per_tile = ts // POOL_HALO
    return pl.pallas_call(
        functools.partial(_pool_body, ts),
        out_shape=jax.ShapeDtypeStruct((b, s, width), BF16),
        grid=(b, s // ts),
        in_specs=[pl.BlockSpec((1, ts, width), lambda bi, i: (bi, i, col)),
                  pl.BlockSpec((1, POOL_HALO, width),
                               lambda bi, i: (bi, jnp.maximum(i * halo_per_tile - 1, 0), col)),
                  pl.BlockSpec(w_pool.shape, lambda bi, i: (0, 0, 0)),
                  pl.BlockSpec((1, width), lambda bi, i: (0, 0))],
        out_specs=pl.BlockSpec((1, ts, width), lambda bi, i: (bi, i, 0)),
        scratch_shapes=[pltpu.VMEM((POOL_HALO + ts, width), F32)],
        compiler_params=_cparams("parallel", "parallel"),
        name="multiscale_pool",
    )(proj, proj, w_pool, pool_scale.reshape(1, width))


def _xattn_body(h_ref, kv_ref, g_ref, wq_ref, qg_ref, kg_ref, wo_ref, o_ref):
    h = h_ref[0]
    d = h.shape[-1]
    kv = kv_ref[0]
    hn = _rms(h, g_ref[...]).astype(BF16)
    q = jnp.dot(hn, wq_ref[...], preferred_element_type=F32)
    scale = XA_HEAD_DIM ** -0.5
    nt = (((1,), (1,)), ((), ()))
    outs = []
    for hh in range(XA_HEADS):
        cols = slice(hh * XA_HEAD_DIM, (hh + 1) * XA_HEAD_DIM)
        qh = _rms(q[:, cols], qg_ref[...]).astype(BF16)
        kh = _rms(kv[:, cols], kg_ref[...]).astype(BF16)
        vh = kv[:, d + hh * XA_HEAD_DIM:d + (hh + 1) * XA_HEAD_DIM].astype(BF16)
        s = lax.dot_general(qh, kh, nt, preferred_element_type=F32) * scale
        m = jnp.max(s, axis=-1, keepdims=True)
        p = jnp.exp(s - m)
        p = p / jnp.sum(p, axis=-1, keepdims=True)
        outs.append(jnp.dot(p.astype(BF16), vh, preferred_element_type=F32))
    o = jnp.concatenate(outs, axis=1).astype(BF16)
    o_ref[0] = h + jnp.dot(o, wo_ref[...], preferred_element_type=F32)


def _cross_attention(h, kv, gain, wq, q_gain, k_gain, wo, ts):
    b, s, d = h.shape
    m = kv.shape[1]
    return pl.pallas_call(
        _xattn_body,
        out_shape=jax.ShapeDtypeStruct((b, s, d), F32),
        grid=(b, s // ts),
        in_specs=[pl.BlockSpec((1, ts, d), lambda bi, i: (bi, i, 0)),
                  pl.BlockSpec((1, m, 2 * d), lambda bi, i: (bi, 0, 0)),
                  pl.BlockSpec((1, d), lambda bi, i: (0, 0)),
                  pl.BlockSpec((d, d), lambda bi, i: (0, 0)),
                  pl.BlockSpec((1, XA_HEAD_DIM), lambda bi, i: (0, 0)),
                  pl.BlockSpec((1, XA_HEAD_DIM), lambda bi, i: (0, 0)),
                  pl.BlockSpec((d, d), lambda bi, i: (0, 0))],
        out_specs=pl.BlockSpec((1, ts, d), lambda bi, i: (bi, i, 0)),
        compiler_params=_cparams("parallel", "parallel"),
        name="memory_cross_attention",
    )(h, kv, gain.reshape(1, d), wq, q_gain.reshape(1, -1), k_gain.reshape(1, -1), wo)


def _swiglu_body(h_ref, g_ref, wg_ref, wu_ref, wd_ref, o_ref):
    h = h_ref[...]
    hn = _rms(h, g_ref[...]).astype(BF16)
    f = wg_ref.shape[1]
    acc = h
    for c0 in range(0, f, FF_CHUNK):
        cols = slice(c0, min(c0 + FF_CHUNK, f))
        gate = jnp.dot(hn, wg_ref[:, cols], preferred_element_type=F32)
        up = jnp.dot(hn, wu_ref[:, cols], preferred_element_type=F32)
        act = (_silu(gate) * up).astype(BF16)
        acc = acc + jnp.dot(act, wd_ref[cols, :], preferred_element_type=F32)
    o_ref[...] = acc


def _swiglu(h, gain, wg, wu, wd, tm):
    n, d = h.shape
    f = wg.shape[1]
    return pl.pallas_call(
        _swiglu_body,
        out_shape=jax.ShapeDtypeStruct((n, d), F32),
        grid=(n // tm,),
        in_specs=[pl.BlockSpec((tm, d), lambda i: (i, 0)),
                  pl.BlockSpec((1, d), lambda i: (0, 0)),
                  pl.BlockSpec((d, f), lambda i: (0, 0)),
                  pl.BlockSpec((d, f), lambda i: (0, 0)),
                  pl.BlockSpec((f, d), lambda i: (0, 0))],
        out_specs=pl.BlockSpec((tm, d), lambda i: (i, 0)),
        compiler_params=_cparams("parallel"),
        name="swiglu",
    )(h, gain.reshape(1, d), wg, wu, wd)


def _ssd_body(zx_ref, cw_ref, cb_ref, dtb_ref, alog_ref, dskip_ref, ng_ref, o_ref,
              prev_ref, state_ref):
    c = pl.program_id(1)
    L = SSD_CHUNK
    n_state = SSD_STATE
    pair_w = 2 * SSD_HEAD_DIM
    assert pair_w == LANES and n_state == LANES and L == LANES

    @pl.when(c == 0)
    def _():
        prev_ref[...] = jnp.zeros_like(prev_ref)
        state_ref[...] = jnp.zeros_like(state_ref)

    blk = zx_ref[0]
    z = blk[:, :SSD_D_INNER]
    xr = blk[:, SSD_D_INNER:SSD_D_INNER + SSD_CONV_DIM]
    dtr = blk[:, SSD_D_INNER + SSD_CONV_DIM:SSD_D_INNER + SSD_CONV_DIM + LANES]

    tail = prev_ref[...]
    head = xr[0:SUBLANES]
    row = _iota((SUBLANES, 1), 0)
    conv = xr * cw_ref[SSD_CONV - 1:SSD_CONV, :]
    conv_head = head * cw_ref[SSD_CONV - 1:SSD_CONV, :]
    for k in range(1, SSD_CONV):
        tap = cw_ref[SSD_CONV - 1 - k:SSD_CONV - k, :]
        conv = conv + pltpu.roll(xr, k, 0) * tap
        conv_head = conv_head + jnp.where(row < k, pltpu.roll(tail, k, 0),
                                          pltpu.roll(head, k, 0)) * tap
    conv = jnp.concatenate([conv_head, conv[SUBLANES:]], axis=0)
    prev_ref[...] = xr[L - SUBLANES:L]
    xa = _silu(conv + cb_ref[...])
    xs = xa[:, :SSD_D_INNER]
    bm = xa[:, SSD_D_INNER:SSD_D_INNER + SSD_GROUPS * n_state]
    cm = xa[:, SSD_D_INNER + SSD_GROUPS * n_state:]

    dtx = dtr + dtb_ref[...]
    dt = jnp.maximum(dtx, 0.0) + jnp.log(1.0 + jnp.exp(-jnp.abs(dtx)))
    da = dt * (-jnp.exp(alog_ref[...]))
    tri = jnp.where(_iota((L, L), 1) <= _iota((L, L), 0), 1.0, 0.0)
    acum = jnp.dot(tri, da, precision=HIGHEST, preferred_element_type=F32)
    acum_t = acum.T
    dt_t = dt.T
    shifted_t = acum_t - jnp.log(dt_t)
    to_end_t = jnp.exp(acum_t[:, L - 1:L] - acum_t)
    causal = _iota((L, L), 1) <= _iota((L, L), 0)
    lane = _iota((1, LANES), 1)
    low = lane < SSD_HEAD_DIM
    nt = (((1,), (1,)), ((), ()))

    heads_per_group = SSD_HEADS // SSD_GROUPS
    y_pairs = []
    for g in range(SSD_GROUPS):
        bg = bm[:, g * n_state:(g + 1) * n_state]
        cg = cm[:, g * n_state:(g + 1) * n_state].astype(BF16)
        bg_t = bg.T
        cb = lax.dot_general(cg, bg.astype(BF16), nt, preferred_element_type=F32)
        for pp in range(heads_per_group // 2):
            pidx = g * (heads_per_group // 2) + pp
            cols = slice(pidx * pair_w, (pidx + 1) * pair_w)
            x_pair = xs[:, cols].astype(BF16)
            st_pair = state_ref[:, cols]
            cs = jnp.dot(cg, st_pair.astype(BF16), preferred_element_type=F32)
            ys, upds, lasts = [], [], []
            for hh in range(2):
                h = 2 * pidx + hh
                bc = jnp.broadcast_to(acum[:, h:h + 1], (L, L))
                seg = bc - shifted_t[h:h + 1, :]
                dec = jnp.exp(jnp.where(causal, seg, -jnp.inf))
                mm = (cb * dec).astype(BF16)
                ebc = jnp.exp(bc)
                ys.append(jnp.dot(mm, x_pair, preferred_element_type=F32) + ebc * cs)
                wrow = dt_t[h:h + 1, :] * to_end_t[h:h + 1, :]
                upds.append(jnp.dot((bg_t * wrow).astype(BF16), x_pair,
                                    preferred_element_type=F32))
                lasts.append(ebc[L - 1:L, :])
            y_pairs.append(jnp.where(low, ys[0], ys[1]))
            state_ref[:, cols] = (st_pair * jnp.where(low, lasts[0], lasts[1])
                                  + jnp.where(low, upds[0], upds[1]))

    y = jnp.concatenate(y_pairs, axis=1)
    y = (y + dskip_ref[...] * xs) * _silu(z)
    gw = SSD_D_INNER // SSD_GROUPS
    outs = []
    for g in range(SSD_GROUPS):
        cols = slice(g * gw, (g + 1) * gw)
        outs.append(_rms(y[:, cols], ng_ref[:, cols]))
    o_ref[0] = jnp.concatenate(outs, axis=1).astype(o_ref.dtype)


def _ssd_mixer(zx, conv_w, conv_b, dt_bias, a_log, d_skip, norm_g):
    b, s, f = zx.shape
    pad = LANES - SSD_HEADS
    dtb = jnp.pad(dt_bias, (0, pad)).reshape(1, LANES)
    alog = jnp.pad(a_log, (0, pad)).reshape(1, LANES)
    dskip = jnp.repeat(d_skip, SSD_HEAD_DIM).reshape(1, SSD_D_INNER)
    full = lambda shape: pl.BlockSpec(shape, lambda bi, c: (0,) * len(shape))
    return pl.pallas_call(
        _ssd_body,
        out_shape=jax.ShapeDtypeStruct((b, s, SSD_D_INNER), BF16),
        grid=(b, s // SSD_CHUNK),
        in_specs=[pl.BlockSpec((1, SSD_CHUNK, f), lambda bi, c: (bi, c, 0)),
                  full((SSD_CONV, SSD_CONV_DIM)), full((1, SSD_CONV_DIM)),
                  full((1, LANES)), full((1, LANES)),
                  full((1, SSD_D_INNER)), full((1, SSD_D_INNER))],
        out_specs=pl.BlockSpec((1, SSD_CHUNK, SSD_D_INNER), lambda bi, c: (bi, c, 0)),
        scratch_shapes=[pltpu.VMEM((SUBLANES, SSD_CONV_DIM), F32),
                        pltpu.VMEM((SSD_STATE, SSD_D_INNER), F32)],
        compiler_params=_cparams("parallel", "arbitrary"),
        name="ssd_mixer",
    )(zx, conv_w, conv_b.reshape(1, -1), dtb, alog, dskip, norm_g.reshape(1, -1))


def _router_body(h_ref, g_ref, wr_ref, hn_ref, lg_ref):
    hn = _rms(h_ref[...], g_ref[...])
    _matrix_to_rows(hn_ref, hn)
    lg_ref[...] = jnp.dot(hn, wr_ref[...], precision=HIGHEST, preferred_element_type=F32)


def _router(h, gain, w_router_pad, tm):
    n, d = h.shape
    assert d == SUBLANES * LANES
    hn_rows, logits = pl.pallas_call(
        _router_body,
        out_shape=[jax.ShapeDtypeStruct((n * SUBLANES, LANES), F32),
                   jax.ShapeDtypeStruct((n, LANES), F32)],
        grid=(n // tm,),
        in_specs=[pl.BlockSpec((tm, d), lambda i: (i, 0)),
                  pl.BlockSpec((1, d), lambda i: (0, 0)),
                  pl.BlockSpec((d, LANES), lambda i: (0, 0))],
        out_specs=[pl.BlockSpec((tm * SUBLANES, LANES), lambda i: (i, 0)),
                   pl.BlockSpec((tm, LANES), lambda i: (i, 0))],
        compiler_params=_cparams("parallel"),
        name="moe_router",
    )(h, gain.reshape(1, d), w_router_pad)
    return hn_rows.reshape(n, SUBLANES, LANES), logits


def _matrix_to_rows(rows_ref, x):
    n = x.shape[0]
    for j in range(x.shape[1] // LANES):
        rows_ref[pl.ds(j, n, stride=SUBLANES), :] = x[:, j * LANES:(j + 1) * LANES]


def _row_copy(src_hbm, dst_ref, sems, slot, idx, r):
    rows = pl.ds(pl.multiple_of(r * SUBLANES, SUBLANES), SUBLANES)
    return pltpu.make_async_copy(src_hbm.at[idx], dst_ref.at[slot, rows, :], sems.at[slot])


def _start_row_gather(idx_ref, src_hbm, dst_ref, sems, slot, n_rows):
    def body(r, carry):
        _row_copy(src_hbm, dst_ref, sems, slot, idx_ref[0, 0, r], r).start()
        return carry

    lax.fori_loop(0, n_rows, body, 0, unroll=8)


def _wait_row_gather(src_hbm, dst_ref, sems, slot, n_rows):
    def body(r, carry):
        _row_copy(src_hbm, dst_ref, sems, slot, 0, r).wait()
        return carry

    lax.fori_loop(0, n_rows, body, 0, unroll=8)


def _rows_to_matrix(buf_ref, slot, first_row, n_rows, d):
    chunks = [buf_ref[slot, pl.ds(first_row * SUBLANES + j, n_rows, stride=SUBLANES), :]
              for j in range(d // LANES)]
    return jnp.concatenate(chunks, axis=1)


def _expert_body(blk_e_ref, n_valid_ref, tok_ref, tok_next_ref, hn_hbm, gate_ref,
                 wg_ref, wu_ref, wd_ref, o_ref, x_ref, sems):
    i = pl.program_id(0)
    last = pl.num_programs(0) - 1
    rows = gate_ref.shape[0]
    d = wg_ref.shape[1]
    slot = i % 2

    @pl.when(jnp.logical_and(i == 0, n_valid_ref[0] > 0))
    def _():
        _start_row_gather(tok_ref, hn_hbm, x_ref, sems, 0, rows)

    @pl.when(jnp.logical_and(i < last, n_valid_ref[jnp.minimum(i + 1, last)] > 0))
    def _():
        _start_row_gather(tok_next_ref, hn_hbm, x_ref, sems, 1 - slot, rows)

    @pl.when(n_valid_ref[i] > 0)
    def _():
        _wait_row_gather(hn_hbm, x_ref, sems, slot, rows)
        x = _rows_to_matrix(x_ref, slot, 0, rows, d).astype(BF16)
        f = wg_ref.shape[2]
        acc = jnp.zeros((rows, d), F32)
        for c0 in range(0, f, FF_CHUNK):
            cols = slice(c0, min(c0 + FF_CHUNK, f))
            gate = jnp.dot(x, wg_ref[0, :, cols], preferred_element_type=F32)
            up = jnp.dot(x, wu_ref[0, :, cols], preferred_element_type=F32)
            act = (_silu(gate) * up).astype(BF16)
            acc = acc + jnp.dot(act, wd_ref[0, cols, :], preferred_element_type=F32)
        _matrix_to_rows(o_ref, acc * gate_ref[...])

    @pl.when(n_valid_ref[i] == 0)
    def _():
        o_ref[...] = jnp.zeros_like(o_ref)


def _expert_ffn(blk_e, n_valid, row_tok, hn_rows, row_gate, wg, wu, wd):
    n_blocks = blk_e.shape[0]
    d = wg.shape[1]
    f = wg.shape[2]
    rows = MOE_ROWS
    tok = row_tok.reshape(n_blocks, 1, rows)
    grid_spec = pltpu.PrefetchScalarGridSpec(
        num_scalar_prefetch=2,
        grid=(n_blocks,),
        in_specs=[pl.BlockSpec((1, 1, rows), lambda i, be, nv: (i, 0, 0), memory_space=pltpu.SMEM),
                  pl.BlockSpec((1, 1, rows), lambda i, be, nv: (jnp.minimum(i + 1, n_blocks - 1), 0, 0),
                               memory_space=pltpu.SMEM),
                  pl.BlockSpec(memory_space=pl.ANY),
                  pl.BlockSpec((rows, 1), lambda i, be, nv: (i, 0)),
                  pl.BlockSpec((1, d, f), lambda i, be, nv: (be[i], 0, 0)),
                  pl.BlockSpec((1, d, f), lambda i, be, nv: (be[i], 0, 0)),
                  pl.BlockSpec((1, f, d), lambda i, be, nv: (be[i], 0, 0))],
        out_specs=pl.BlockSpec((rows * SUBLANES, LANES), lambda i, be, nv: (i, 0)),
        scratch_shapes=[pltpu.VMEM((2, rows * SUBLANES, LANES), F32),
                        pltpu.SemaphoreType.DMA((2,))],
    )
    y_rows = pl.pallas_call(
        _expert_body,
        out_shape=jax.ShapeDtypeStruct((n_blocks * rows * SUBLANES, LANES), F32),
        grid_spec=grid_spec,
        compiler_params=_cparams("arbitrary"),
        name="moe_expert_ffn",
    )(blk_e, n_valid, tok, tok, hn_rows, row_gate.reshape(-1, 1), wg, wu, wd)
    return y_rows.reshape(n_blocks * rows, SUBLANES, LANES)


def _combine_body(pos_ref, pos_next_ref, h_ref, y_hbm, o_ref, y_ref, sems):
    i = pl.program_id(0)
    last = pl.num_programs(0) - 1
    tc, d = o_ref.shape
    slot = i % 2

    @pl.when(i == 0)
    def _():
        _start_row_gather(pos_ref, y_hbm, y_ref, sems, 0, TOP_K * tc)

    @pl.when(i < last)
    def _():
        _start_row_gather(pos_next_ref, y_hbm, y_ref, sems, 1 - slot, TOP_K * tc)

    _wait_row_gather(y_hbm, y_ref, sems, slot, TOP_K * tc)
    acc = h_ref[...]
    for k in range(TOP_K):
        acc = acc + _rows_to_matrix(y_ref, slot, k * tc, tc, d)
    o_ref[...] = acc


def _moe_combine(h, pos, y_rows):
    n, d = h.shape
    tc = COMBINE_TOKENS
    n_tiles = n // tc
    pos_tiles = pos.reshape(n_tiles, tc, TOP_K).transpose(0, 2, 1).reshape(n_tiles, 1, TOP_K * tc)
    return pl.pallas_call(
        _combine_body,
        out_shape=jax.ShapeDtypeStruct((n, d), F32),
        grid=(n_tiles,),
        in_specs=[pl.BlockSpec((1, 1, TOP_K * tc), lambda i: (i, 0, 0), memory_space=pltpu.SMEM),
                  pl.BlockSpec((1, 1, TOP_K * tc), lambda i: (jnp.minimum(i + 1, n_tiles - 1), 0, 0),
                               memory_space=pltpu.SMEM),
                  pl.BlockSpec((tc, d), lambda i: (i, 0)),
                  pl.BlockSpec(memory_space=pl.ANY)],
        out_specs=pl.BlockSpec((tc, d), lambda i: (i, 0)),
        scratch_shapes=[pltpu.VMEM((2, TOP_K * tc * SUBLANES, LANES), F32),
                        pltpu.SemaphoreType.DMA((2,))],
        compiler_params=_cparams("arbitrary"),
        name="moe_combine",
    )(pos_tiles, pos_tiles, h, y_rows)


def _moe_layer(h, gain, w_router, wg, wu, wd, tm):
    n, d = h.shape
    e = w_router.shape[1]
    rows = MOE_ROWS
    n_slots = n * TOP_K
    hn_rows, logits = _router(h, gain, jnp.pad(w_router, ((0, 0), (0, LANES - e))), tm)
    top_logit, top_e = lax.top_k(logits[:, :e], TOP_K)
    gates = jax.nn.softmax(top_logit, axis=-1)

    e_flat = top_e.reshape(-1).astype(jnp.int32)
    slot_ids = jnp.arange(n_slots, dtype=jnp.int32)
    e_sorted, order = lax.sort((e_flat, slot_ids), num_keys=1, is_stable=True)
    experts = jnp.arange(e, dtype=jnp.int32)
    sorted_onehot = e_sorted[:, None] == experts[None, :]
    counts = jnp.sum(sorted_onehot, axis=0, dtype=jnp.int32)
    start = jnp.cumsum(counts) - counts
    padded = (counts + rows - 1) // rows * rows
    pad_end = jnp.cumsum(padded)
    pad_start = pad_end - padded
    shift = jnp.sum(jnp.where(sorted_onehot, (pad_start - start)[None, :], 0), axis=1)
    dest = slot_ids + shift
    _, pos = lax.sort((order, dest), num_keys=1)
    n_blocks = n_slots // rows + e
    n_rows = n_blocks * rows
    blk_start = jnp.arange(n_blocks, dtype=jnp.int32) * rows
    blk_e = jnp.minimum(jnp.sum(blk_start[:, None] >= pad_end[None, :], axis=1),
                        e - 1).astype(jnp.int32)
    n_valid = jnp.clip(pad_start[blk_e] + counts[blk_e] - blk_start, 0, rows).astype(jnp.int32)
    row_rank = (jnp.arange(rows, dtype=jnp.int32)[None, :] + (blk_start - pad_start[blk_e])[:, None])
    row_ok = row_rank < counts[blk_e][:, None]
    src = jnp.where(row_ok, start[blk_e][:, None] + row_rank, 0).reshape(-1)
    row_slot = order[src]
    row_tok = jnp.where(row_ok.reshape(-1), row_slot // TOP_K, 0)
    row_gate = jnp.where(row_ok.reshape(-1), gates.reshape(-1)[row_slot], 0.0)

    y_rows = _expert_ffn(blk_e, n_valid, row_tok, hn_rows, row_gate, wg, wu, wd)
    return _moe_combine(h, pos, y_rows)


def _rope_tables(positions):
    half = ATT_HEAD_DIM // 2
    inv_freq = ROPE_THETA ** (-jnp.arange(half, dtype=F32) / half)
    ang = positions.astype(F32)[:, None] * inv_freq[None, :]
    cos = jnp.cos(ang)
    sin = jnp.sin(ang)
    reps = LANES // ATT_HEAD_DIM
    return (jnp.tile(cos, (1, 2 * reps)), jnp.tile(jnp.concatenate([-sin, sin], axis=1), (1, reps)))


def kernel(x, mem, positions, mix_norm, xa_norm, mem_norm, ffn_norm, xa_wq, xa_wkv, xa_q_norm, xa_k_norm, xa_wo, hy_w_in, hy_q_norm, hy_k_norm, pool_w, pool_scale, hy_w_out, ffn_w_gate, ffn_w_up, ffn_w_down, ssd_w_in, ssd_conv_w, ssd_conv_b, ssd_dt_bias, ssd_a_log, ssd_d, ssd_norm, ssd_w_out, moe_router, moe_w_gate, moe_w_up, moe_w_down):
    b, s, d = x.shape
    n = b * s
    n_mem = mem.shape[1]
    depth = mix_norm.shape[0]
    tm = min(512, n)
    ts = min(512, s)
    bf = lambda w: w.astype(BF16)
    cos, sin = _rope_tables(positions)
    reps = LANES // ATT_HEAD_DIM

    h = x.reshape(n, d)
    mem2 = mem.reshape(b * n_mem, d)
    for layer in range(depth):
        j = layer // 2
        if layer % 2 == 0:
            proj = _norm_matmul(h, mix_norm[layer], bf(hy_w_in[j]), min(1024, n), 2048).reshape(b, s, -1)
            qn, kn, vt, kmean = _qkv_prep(proj, cos, sin,
                                          jnp.tile(hy_q_norm[j], reps).reshape(1, LANES),
                                          jnp.tile(hy_k_norm[j], reps).reshape(1, LANES))
            att = _moba_attention(qn, kn, vt, _moba_select(qn, kmean))
            pooled = _multiscale_pool(proj, bf(pool_w[j]), pool_scale[j], ts)
            w_out = bf(hy_w_out[j])
            h = _matmul_residual(h, [(att.reshape(n, -1), w_out[:ATT_WIDTH]),
                                     (pooled.reshape(n, -1), w_out[ATT_WIDTH:])], min(1024, n))
        else:
            w_in = ssd_w_in[j]
            w_cat = jnp.pad(w_in, ((0, 0), (0, SSD_DT_PAD - SSD_HEADS)))
            zx = _norm_matmul(h, mix_norm[layer], bf(w_cat), min(1024, n), SSD_PROJ_TILE).reshape(b, s, -1)
            y = _ssd_mixer(zx, ssd_conv_w[j], ssd_conv_b[j], ssd_dt_bias[j], ssd_a_log[j],
                           ssd_d[j], ssd_norm[j])
            h = _matmul_residual(h, [(y.reshape(n, -1), bf(ssd_w_out[j]))], min(1024, n))

        kv = _norm_matmul(mem2, mem_norm[layer], bf(xa_wkv[layer]), min(512, b * n_mem), 1024)
        h = _cross_attention(h.reshape(b, s, d), kv.reshape(b, n_mem, 2 * d), xa_norm[layer],
                             bf(xa_wq[layer]), xa_q_norm[layer], xa_k_norm[layer],
                             bf(xa_wo[layer]), min(1024, s)).reshape(n, d)

        if layer % 2 == 0:
            h = _swiglu(h, ffn_norm[layer], bf(ffn_w_gate[j]), bf(ffn_w_up[j]),
                        bf(ffn_w_down[j]), tm)
        else:
            h = _moe_layer(h, ffn_norm[layer], moe_router[j], bf(moe_w_gate[j]),
                           bf(moe_w_up[j]), bf(moe_w_down[j]), tm)
    return h.reshape(b, s, d)
```
